```python
import math
import jax, jax.numpy as jnp
from jax import lax
import numpy as np

D_MODEL = 1024
BATCH = 2
SEQ = 16384
DEPTH = 4

HEAD_DIM = 64
GRID_W = 64
Q_BLOCK = 128
RMS_EPS = 1e-6
LN_EPS = 1e-5
A_HEADS = 4
A_KV_HEADS = 2
A_GROUP = A_HEADS // A_KV_HEADS
ROPE_THETA = 10000.0
B_HEADS = 4
B_QK_DIM = HEAD_DIM // 2
C_HEADS = 4
NA_ROWS = 8
NA_COLS = 16
D_GROUPS = 4
D_CHUNK = 128
D_WIDTH = D_GROUPS * HEAD_DIM
FFN_DIM = -(-8 * D_MODEL // (3 * 256)) * 256
PLE_DIM = 256

A_Q = A_HEADS * HEAD_DIM
A_KV = A_KV_HEADS * HEAD_DIM
B_QK = B_HEADS * 2 * B_QK_DIM
B_V = B_HEADS * HEAD_DIM
C_W = C_HEADS * HEAD_DIM
MIX_WIDTH = A_Q + B_V + C_W + D_WIDTH
PROJ_SPLITS = (A_Q, A_KV, A_KV, B_QK, B_QK, B_V, C_W, C_W, C_W, 2 * D_WIDTH)
PROJ_WIDTH = A_Q + 2 * A_KV + 2 * B_QK + B_V + 3 * C_W + 2 * D_WIDTH

kernel_name = 'hybrid_parallel_mixer_encoder'


def rms_norm(x, g, eps=RMS_EPS):
    xf = x.astype(jnp.float32)
    y = xf * lax.rsqrt(jnp.mean(xf * xf, axis=-1, keepdims=True) + eps)
    return (y * g.astype(jnp.float32)).astype(x.dtype)


def layer_norm(x, g, b, eps=LN_EPS):
    xf = x.astype(jnp.float32)
    mu = jnp.mean(xf, axis=-1, keepdims=True)
    xc = xf - mu
    y = xc * lax.rsqrt(jnp.mean(xc * xc, axis=-1, keepdims=True) + eps)
    return (y * g.astype(jnp.float32) + b.astype(jnp.float32)).astype(x.dtype)


def split_cols(y, sizes):
    outs, start = [], 0
    for n in sizes:
        outs.append(y[..., start:start + n])
        start += n
    return outs


def to_blocks(x):
    b, s = x.shape[:2]
    return jnp.swapaxes(x.reshape(b, s // Q_BLOCK, Q_BLOCK, *x.shape[2:]), 0, 1)


def from_blocks(y):
    y = jnp.swapaxes(y, 0, 1)
    return y.reshape(y.shape[0], -1, *y.shape[3:])


def axial_rope_tables(seq_len):
    t = jnp.arange(seq_len)
    row = (t // GRID_W).astype(jnp.float32)
    col = (t % GRID_W).astype(jnp.float32)
    n_freq = HEAD_DIM // 4
    inv = ROPE_THETA ** (-jnp.arange(n_freq, dtype=jnp.float32) / n_freq)
    ang = jnp.concatenate([row[:, None] * inv, col[:, None] * inv], axis=-1)
    return jnp.cos(ang), jnp.sin(ang)


def apply_rope(x, cos, sin):
    xf = x.astype(jnp.float32)
    half = HEAD_DIM // 2
    x1, x2 = xf[..., :half], xf[..., half:]
    c, s = cos[None, :, None, :], sin[None, :, None, :]
    return jnp.concatenate([x1 * c - x2 * s, x2 * c + x1 * s], axis=-1).astype(x.dtype)


def alibi_slopes(n):
    start = 2.0 ** (-8.0 / n)
    return start ** jnp.arange(1, n + 1, dtype=jnp.float32)


def gqa_axial_attention(q, k, v):
    b, s, _, dh = q.shape
    cos, sin = axial_rope_tables(s)
    q = apply_rope(q, cos, sin)
    k = apply_rope(k, cos, sin)
    qb = to_blocks(q.reshape(b, s, A_KV_HEADS, A_GROUP, dh) * (dh ** -0.5))

    def step(q_blk):
        sc = jnp.einsum('bqkgd,bskd->bkgqs', q_blk, k).astype(jnp.float32)
        pr = jax.nn.softmax(sc, axis=-1).astype(v.dtype)
        return jnp.einsum('bkgqs,bskd->bqkgd', pr, v)

    o = from_blocks(lax.map(step, qb))
    return o.reshape(b, s, A_HEADS * dh)


def diff_attention(q, k, v, lam, lam_init, g_sub):
    b, s = q.shape[:2]
    pos = jnp.arange(s, dtype=jnp.float32)
    slopes = alibi_slopes(B_HEADS)
    qb = to_blocks(q * (B_QK_DIM ** -0.5))
    pb = pos.reshape(s // Q_BLOCK, Q_BLOCK)

    def step(args):
        q_blk, q_pos = args
        sc = jnp.einsum('bqhmd,bshmd->bhmqs', q_blk, k).astype(jnp.float32)
        dist = jnp.abs(q_pos[:, None] - pos[None, :])
        sc = sc - slopes[None, :, None, None, None] * dist[None, None, None]
        pr = jax.nn.softmax(sc, axis=-1)
        a = pr[:, :, 0] - lam * pr[:, :, 1]
        return jnp.einsum('bhqs,bshd->bqhd', a.astype(v.dtype), v)

    o = from_blocks(lax.map(step, (qb, pb)))
    o = rms_norm(o, g_sub) * (1.0 - lam_init)
    return o.reshape(b, s, B_HEADS * HEAD_DIM)


def neighbourhood_attention(q, k, v, rpb):
    b, s, h, dh = q.shape
    rows = s // GRID_W
    wr = min(NA_ROWS, rows)
    qg = q.reshape(b, rows, GRID_W, h, dh).transpose(1, 0, 3, 2, 4)
    kg = k.reshape(b, rows, GRID_W, h, dh).transpose(0, 3, 1, 2, 4)
    vg = v.reshape(b, rows, GRID_W, h, dh).transpose(0, 3, 1, 2, 4)
    cols = jnp.arange(GRID_W)
    cs = jnp.clip(cols - NA_COLS // 2, 0, GRID_W - NA_COLS)
    col_idx = cs[:, None] + jnp.arange(NA_COLS)[None, :]
    dc = col_idx - cols[:, None] + (NA_COLS - 1)
    scale = dh ** -0.5

    def row_block(args):
        r, q_r = args
        rs = jnp.clip(r - wr // 2, 0, rows - wr)
        k_band = lax.dynamic_slice_in_dim(kg, rs, wr, axis=2)
        v_band = lax.dynamic_slice_in_dim(vg, rs, wr, axis=2)
        k_nb = k_band[:, :, :, col_idx, :]
        v_nb = v_band[:, :, :, col_idx, :]
        sc = jnp.einsum('bhqd,bhrqcd->bhqrc', q_r * scale, k_nb).astype(jnp.float32)
        dr = rs + jnp.arange(wr) - r + (NA_ROWS - 1)
        bias = rpb[:, dr[:, None, None], dc[None, :, :]]
        sc = sc + bias.transpose(0, 2, 1, 3).astype(jnp.float32)[None]
        pr = jax.nn.softmax(sc.reshape(b, h, GRID_W, wr * NA_COLS), axis=-1)
        pr = pr.reshape(b, h, GRID_W, wr, NA_COLS).astype(v.dtype)
        return jnp.einsum('bhqrc,bhrqcd->bhqd', pr, v_nb)

    o = lax.map(row_block, (jnp.arange(rows), qg))
    return o.transpose(1, 0, 3, 2, 4).reshape(b, s, h * dh)


def spatial_gating(uv, ln_g, ln_b, w_s, b_s):
    z = jax.nn.gelu(uv, approximate=False)
    u, vv = z[..., :D_WIDTH], z[..., D_WIDTH:]
    vv = layer_norm(vv, ln_g, ln_b)
    b, s, c = vv.shape
    vv = vv.reshape(b, s // D_CHUNK, D_CHUNK, D_GROUPS, c // D_GROUPS)
    sv = jnp.einsum('gts,bnsgc->bntgc', w_s, vv) + b_s.T[None, None, :, :, None]
    return u * sv.reshape(b, s, c)


def setup_inputs(seed: int = 0) -> dict:
    key = jax.random.key(seed)
    ks = jax.random.split(key, 24)
    f32 = jnp.float32
    nrm = lambda k, shape, sc: jax.random.normal(k, shape, f32) * sc
    gain = lambda k, shape: 1.0 + 0.02 * jax.random.normal(k, shape, f32)
    L = DEPTH
    return {
        'x': nrm(ks[0], (BATCH, SEQ, D_MODEL), 1.0),
        'p': nrm(ks[1], (DEPTH, BATCH, SEQ, PLE_DIM), 1.0),
        'g_mix': gain(ks[2], (L, D_MODEL)),
        'w_in': nrm(ks[3], (L, D_MODEL, PROJ_WIDTH), D_MODEL ** -0.5),
        'a_q_norm': gain(ks[4], (L, HEAD_DIM)),
        'a_k_norm': gain(ks[5], (L, HEAD_DIM)),
        'b_lam_q': nrm(ks[6], (L, 2, B_QK_DIM), 0.1),
        'b_lam_k': nrm(ks[7], (L, 2, B_QK_DIM), 0.1),
        'b_sub_norm': gain(ks[8], (L, HEAD_DIM)),
        'c_rpb': nrm(ks[9], (L, C_HEADS, 2 * NA_ROWS - 1, 2 * NA_COLS - 1), 0.1),
        'd_ln_g': gain(ks[10], (L, D_WIDTH)),
        'd_ln_b': nrm(ks[11], (L, D_WIDTH), 0.02),
        'd_w_s': nrm(ks[12], (L, D_GROUPS, D_CHUNK, D_CHUNK), D_CHUNK ** -0.5),
        'd_b_s': 1.0 + nrm(ks[13], (L, D_GROUPS, D_CHUNK), 0.1),
        'w_out': nrm(ks[14], (L, MIX_WIDTH, D_MODEL), MIX_WIDTH ** -0.5),
        'g_ffn': gain(ks[15], (L, D_MODEL)),
        'w_gate': nrm(ks[16], (L, D_MODEL, FFN_DIM), D_MODEL ** -0.5),
        'w_up': nrm(ks[17], (L, D_MODEL, FFN_DIM), D_MODEL ** -0.5),
        'w_down': nrm(ks[18], (L, FFN_DIM, D_MODEL), FFN_DIM ** -0.5),
        'g_ple': gain(ks[19], (L, D_MODEL)),
        'w_ple_gate': nrm(ks[20], (L, D_MODEL, D_MODEL), D_MODEL ** -0.5),
        'w_ple_proj': nrm(ks[21], (L, PLE_DIM, D_MODEL), PLE_DIM ** -0.5),
        'g_final': gain(ks[22], (D_MODEL,)),
    }


def reference(x, p, g_mix, w_in, a_q_norm, a_k_norm, b_lam_q, b_lam_k, b_sub_norm,
              c_rpb, d_ln_g, d_ln_b, d_w_s, d_b_s, w_out, g_ffn, w_gate, w_up, w_down,
              g_ple, w_ple_gate, w_ple_proj, g_final):
    b, s, _ = x.shape
    h = x
    for i in range(DEPTH):
        hn = rms_norm(h, g_mix[i])
        proj = hn @ w_in[i]
        aq, ak, av, bq, bk, bv, cq, ck, cv, duv = split_cols(proj, PROJ_SPLITS)

        aq = rms_norm(aq.reshape(b, s, A_HEADS, HEAD_DIM), a_q_norm[i])
        ak = rms_norm(ak.reshape(b, s, A_KV_HEADS, HEAD_DIM), a_k_norm[i])
        av = av.reshape(b, s, A_KV_HEADS, HEAD_DIM)
        ya = gqa_axial_attention(aq, ak, av)

        lam_init = 0.8 - 0.6 * math.exp(-0.3 * i)
        lq = b_lam_q[i].astype(jnp.float32)
        lk = b_lam_k[i].astype(jnp.float32)
        lam = jnp.exp(jnp.sum(lq[0] * lk[0])) - jnp.exp(jnp.sum(lq[1] * lk[1])) + lam_init
        yb = diff_attention(bq.reshape(b, s, B_HEADS, 2, B_QK_DIM),
                            bk.reshape(b, s, B_HEADS, 2, B_QK_DIM),
                            bv.reshape(b, s, B_HEADS, HEAD_DIM),
                            lam, lam_init, b_sub_norm[i])

        yc = neighbourhood_attention(cq.reshape(b, s, C_HEADS, HEAD_DIM),
                                     ck.reshape(b, s, C_HEADS, HEAD_DIM),
                                     cv.reshape(b, s, C_HEADS, HEAD_DIM), c_rpb[i])

        yd = spatial_gating(duv, d_ln_g[i], d_ln_b[i], d_w_s[i], d_b_s[i])

        mix = jnp.concatenate([ya, yb, yc, yd], axis=-1)
        h = h + mix @ w_out[i]

        hn = rms_norm(h, g_ffn[i])
        h = h + (jax.nn.silu(hn @ w_gate[i]) * (hn @ w_up[i])) @ w_down[i]

        gate = jax.nn.sigmoid(rms_norm(h, g_ple[i]) @ w_ple_gate[i])
        h = h + gate * (p[i] @ w_ple_proj[i])
    return rms_norm(h, g_final)
```

```python
import functools
import math

import jax
import jax.numpy as jnp
from jax import lax
from jax.experimental import pallas as pl
from jax.experimental.pallas import tpu as pltpu

F32 = jnp.float32
BF16 = jnp.bfloat16

HEAD_DIM = 64
GRID_W = 64
RMS_EPS = 1e-6
LN_EPS = 1e-5
A_HEADS = 4
A_KV_HEADS = 2
ROPE_THETA = 10000.0
B_HEADS = 4
B_QK_DIM = HEAD_DIM // 2
C_HEADS = 4
NA_ROWS = 8
NA_COLS = 16
D_GROUPS = 4
D_CHUNK = 128
MIXER_WIDTH = 256

LOG2E = math.log2(math.e)
NEG_BIG = -1e30

LANES = 128
VMEM_LIMIT_BYTES = 56 * 1024 * 1024

ROW_TILE = 512
KV_CHUNK = 512
A_Q_TILE = 256
B_Q_TILE = 128
C_BLOCK_ROWS = 8
FFN_CHUNK = 256


def _group_ones(width, group):
    r = lax.broadcasted_iota(jnp.int32, (width, width), 0) // group
    c = lax.broadcasted_iota(jnp.int32, (width, width), 1) // group
    return (r == c).astype(F32)


def _group_sum(x, group):
    return jnp.dot(x, _group_ones(x.shape[-1], group), precision=lax.Precision.HIGHEST,
                   preferred_element_type=F32)


def _rms(x, g):
    ms = jnp.mean(x * x, axis=-1, keepdims=True)
    return x * lax.rsqrt(ms + RMS_EPS) * g


def _proj_kernel(h_ref, gmix_ref, win_ref, gq_ref, gk_ref, cos_ref, sin_ref,
                 lng_ref, lnb_ref, ws_ref, bs_ref,
                 aq_o, ak_o, av_o, bq_o, bk_o, bv_o, cq_o, ck_o, cv_o, yd_o):
    tm = h_ref.shape[0]
    hn = _rms(h_ref[...], gmix_ref[...]).astype(BF16)

    def proj(lo, width):
        return jnp.dot(hn, win_ref[:, lo:lo + width], preferred_element_type=F32)

    cos = cos_ref[...]
    sin = sin_ref[...]

    def qk_norm_rope(y, gain, cos_t, sin_t):
        w = y.shape[-1]
        ss = _group_sum(y * y, HEAD_DIM)
        y = y * lax.rsqrt(ss * (1.0 / HEAD_DIM) + RMS_EPS) * gain
        lane = lax.broadcasted_iota(jnp.int32, y.shape, 1)
        half = HEAD_DIM // 2
        swapped = jnp.where(lane % HEAD_DIM < half,
                            pltpu.roll(y, w - half, axis=1), pltpu.roll(y, half, axis=1))
        return y * cos_t + swapped * sin_t

    a_scale = HEAD_DIM ** -0.5 * LOG2E
    cos2 = jnp.concatenate([cos, cos], axis=1)
    sin2 = jnp.concatenate([sin, sin], axis=1)
    aq = qk_norm_rope(proj(0, 256), gq_ref[...], cos2, sin2) * a_scale
    aq_o[...] = aq.astype(BF16)
    ak = qk_norm_rope(proj(256, 128), gk_ref[...], cos, sin)
    av = proj(384, 128)
    lane128 = lax.broadcasted_iota(jnp.int32, (tm, LANES), 1)
    first = lane128 < HEAD_DIM

    def dup_heads(y):
        r = pltpu.roll(y, HEAD_DIM, axis=1)
        return jnp.concatenate([jnp.where(first, y, r), jnp.where(first, r, y)], axis=1)

    ak_o[...] = dup_heads(ak).astype(BF16)
    av_o[...] = dup_heads(av).astype(BF16)

    bq_o[...] = (proj(512, 256) * (B_QK_DIM ** -0.5 * LOG2E)).astype(BF16)
    bk_o[...] = proj(768, 256).astype(BF16)
    bv_o[...] = proj(1024, 256).astype(BF16)

    cq_o[...] = (proj(1280, 256) * a_scale).astype(BF16)
    ck_o[...] = proj(1536, 256).astype(BF16)
    cv_o[...] = proj(1792, 256).astype(BF16)

    def gelu(t):
        return 0.5 * t * (1.0 + lax.erf(t * (2.0 ** -0.5)))

    u = gelu(proj(2048, 256))
    vv = gelu(proj(2304, 256))
    mu = jnp.mean(vv, axis=-1, keepdims=True)
    vc = vv - mu
    var = jnp.mean(vc * vc, axis=-1, keepdims=True)
    vv = (vc * lax.rsqrt(var + LN_EPS) * lng_ref[...] + lnb_ref[...]).astype(BF16)
    lane_group = lax.broadcasted_iota(jnp.int32, (D_CHUNK, MIXER_WIDTH), 1) // HEAD_DIM
    for n in range(tm // D_CHUNK):
        rows = slice(n * D_CHUNK, (n + 1) * D_CHUNK)
        vchunk = vv[rows, :]
        sv = bs_ref[...]
        for g in range(D_GROUPS):
            mixed = jnp.dot(ws_ref[g], vchunk, preferred_element_type=F32)
            sv = sv + jnp.where(lane_group == g, mixed, 0.0)
        yd_o[rows, :] = (u[rows, :] * sv).astype(BF16)


def _proj_call(h, gmix, win, gq, gk, cos_t, sin_t, lng, lnb, ws, bs, seq):
    rows, d = h.shape
    tm = min(ROW_TILE, seq)
    steps_per_seq = seq // tm
    row_spec = lambda w: pl.BlockSpec((tm, w), lambda i: (i, 0))
    full = lambda a: pl.BlockSpec(a.shape, lambda i: (0,) * a.ndim)
    tab_spec = pl.BlockSpec((tm, LANES), lambda i: (i % steps_per_seq, 0))
    out_sds = jax.ShapeDtypeStruct((rows, MIXER_WIDTH), BF16)
    return pl.pallas_call(
        _proj_kernel,
        grid=(rows // tm,),
        in_specs=[row_spec(d), full(gmix), full(win), full(gq), full(gk), tab_spec, tab_spec,
                  full(lng), full(lnb), full(ws), full(bs)],
        out_specs=[row_spec(MIXER_WIDTH)] * 10,
        out_shape=[out_sds] * 10,
        compiler_params=pltpu.CompilerParams(
            dimension_semantics=("parallel",), vmem_limit_bytes=VMEM_LIMIT_BYTES),
        name="proj_mixers",
    )(h, gmix, win, gq, gk, cos_t, sin_t, lng, lnb, ws, bs)


def _online_softmax_step(s, v, m_scr, l_scr, acc_scr):
    tk = s.shape[1]
    m_prev = m_scr[...]
    m_next = jnp.maximum(m_prev, jnp.max(s, axis=1, keepdims=True))
    alpha = jnp.exp2(m_prev - m_next)
    p = jnp.exp2(s - pltpu.repeat(m_next, tk // LANES, axis=1))
    l_scr[...] = alpha * l_scr[...] + jnp.sum(p, axis=1, keepdims=True)
    acc_scr[...] = alpha * acc_scr[...] + jnp.dot(p.astype(BF16), v, preferred_element_type=F32)
    m_scr[...] = m_next


def _stack_masked(q, groups):
    width = q.shape[1] // groups
    lane = lax.broadcasted_iota(jnp.int32, q.shape, 1) // width
    zero = jnp.zeros_like(q)
    return jnp.concatenate([jnp.where(lane == j, q, zero) for j in range(groups)], axis=0)


def _gqa_kernel(q_ref, k_ref, v_ref, o_ref, qs_scr, m_scr, l_scr, acc_scr, *, tk):
    tq = q_ref.shape[0]
    seq = k_ref.shape[0]
    qs_scr[...] = _stack_masked(q_ref[...], 2)
    m_scr[...] = jnp.full(m_scr.shape, NEG_BIG, F32)
    l_scr[...] = jnp.zeros(l_scr.shape, F32)
    acc_scr[...] = jnp.zeros(acc_scr.shape, F32)

    def body(c, carry):
        start = pl.multiple_of(c * tk, tk)
        k = k_ref[pl.ds(start, tk), :]
        v = v_ref[pl.ds(start, tk), :]
        s = lax.dot_general(qs_scr[...], k, (((1,), (1,)), ((), ())),
                            preferred_element_type=F32)
        _online_softmax_step(s, v, m_scr, l_scr, acc_scr)
        return carry

    lax.fori_loop(0, seq // tk, body, 0)
    o = acc_scr[...] / l_scr[...]
    lane = lax.broadcasted_iota(jnp.int32, (tq, LANES), 1)
    o_ref[...] = jnp.where(lane < HEAD_DIM, o[:tq], o[tq:]).astype(o_ref.dtype)


def _gqa_call(q, k, v):
    b, seq, _ = q.shape
    tq = min(A_Q_TILE, seq)
    tk = min(KV_CHUNK, seq)
    q_spec = pl.BlockSpec((None, tq, LANES), lambda bi, g, qi: (bi, qi, g))
    kv_spec = pl.BlockSpec((None, seq, LANES), lambda bi, g, qi: (bi, 0, g))
    return pl.pallas_call(
        functools.partial(_gqa_kernel, tk=tk),
        grid=(b, A_KV_HEADS, seq // tq),
        in_specs=[q_spec, kv_spec, kv_spec],
        out_specs=q_spec,
        out_shape=jax.ShapeDtypeStruct((b, seq, MIXER_WIDTH), BF16),
        scratch_shapes=[pltpu.VMEM((2 * tq, LANES), BF16),
                        pltpu.VMEM((2 * tq, LANES), F32),
                        pltpu.VMEM((2 * tq, LANES), F32),
                        pltpu.VMEM((2 * tq, LANES), F32)],
        compiler_params=pltpu.CompilerParams(
            dimension_semantics=("parallel", "parallel", "arbitrary"),
            vmem_limit_bytes=VMEM_LIMIT_BYTES),
        name="gqa_attention",
    )(q, k, v)


def _diff_kernel(scal_ref, q_ref, k_ref, v_ref, lq_ref, lk_ref, gsub_ref, o_ref,
                 qs_scr, m_scr, l_scr, acc_scr, *, tk):
    tq = q_ref.shape[0]
    seq = k_ref.shape[0]
    pair = pl.program_id(1)
    qi = pl.program_id(2)
    lam_init = scal_ref[0]
    slope_a = scal_ref[1 + 2 * pair]
    slope_b = scal_ref[2 + 2 * pair]
    qs_scr[...] = _stack_masked(q_ref[...], 4)
    m_scr[...] = jnp.full(m_scr.shape, NEG_BIG, F32)
    l_scr[...] = jnp.zeros(l_scr.shape, F32)
    acc_scr[...] = jnp.zeros(acc_scr.shape, F32)
    q_pos = (qi * tq + lax.broadcasted_iota(jnp.int32, (tq, tk), 0)).astype(F32)
    k_off = lax.broadcasted_iota(jnp.int32, (tq, tk), 1).astype(F32)

    def body(c, carry):
        start = pl.multiple_of(c * tk, tk)
        k = k_ref[pl.ds(start, tk), :]
        v = v_ref[pl.ds(start, tk), :]
        s = lax.dot_general(qs_scr[...], k, (((1,), (1,)), ((), ())),
                            preferred_element_type=F32)
        dist = jnp.abs(q_pos - (k_off + (c * tk).astype(F32)))
        bias_a = dist * slope_a
        bias_b = dist * slope_b
        s = s - jnp.concatenate([bias_a, bias_a, bias_b, bias_b], axis=0)
        _online_softmax_step(s, v, m_scr, l_scr, acc_scr)
        return carry

    lax.fori_loop(0, seq // tk, body, 0)

    t = jnp.exp(jnp.sum(lq_ref[...] * lk_ref[...], axis=1, keepdims=True))
    lam = t[0:1, :] - t[1:2, :] + lam_init
    o = acc_scr[...] / l_scr[...]
    o_a = o[0:tq] - lam * o[tq:2 * tq]
    o_b = o[2 * tq:3 * tq] - lam * o[3 * tq:]
    lane = lax.broadcasted_iota(jnp.int32, (tq, LANES), 1)
    o = jnp.where(lane < HEAD_DIM, o_a, o_b)
    ss = _group_sum(o * o, HEAD_DIM)
    o = o * lax.rsqrt(ss * (1.0 / HEAD_DIM) + RMS_EPS) * gsub_ref[...] * (1.0 - lam_init)
    o_ref[...] = o.astype(o_ref.dtype)


def _diff_call(scal, q, k, v, lq, lk, gsub):
    b, seq, _ = q.shape
    tq = min(B_Q_TILE, seq)
    tk = min(KV_CHUNK, seq)
    q_spec = pl.BlockSpec((None, tq, LANES), lambda bi, g, qi: (bi, qi, g))
    kv_spec = pl.BlockSpec((None, seq, LANES), lambda bi, g, qi: (bi, 0, g))
    full = lambda a: pl.BlockSpec(a.shape, lambda bi, g, qi: (0,) * a.ndim)
    return pl.pallas_call(
        functools.partial(_diff_kernel, tk=tk),
        grid=(b, B_HEADS // 2, seq // tq),
        in_specs=[pl.BlockSpec(memory_space=pltpu.SMEM), q_spec, kv_spec, kv_spec,
                  full(lq), full(lk), full(gsub)],
        out_specs=q_spec,
        out_shape=jax.ShapeDtypeStruct((b, seq, MIXER_WIDTH), BF16),
        scratch_shapes=[pltpu.VMEM((4 * tq, LANES), BF16),
                        pltpu.VMEM((4 * tq, LANES), F32),
                        pltpu.VMEM((4 * tq, LANES), F32),
                        pltpu.VMEM((4 * tq, LANES), F32)],
        compiler_params=pltpu.CompilerParams(
            dimension_semantics=("parallel", "parallel", "arbitrary"),
            vmem_limit_bytes=VMEM_LIMIT_BYTES),
        name="diff_attention",
    )(scal, q, k, v, lq, lk, gsub)


def _nbr_kernel(q_ref, kp_ref, kc_ref, kn_ref, vp_ref, vc_ref, vn_ref, bias_ref, o_ref,
                kband, vband, *, grid_rows):
    blk = q_ref.shape[0]
    rb = blk // GRID_W
    i = pl.program_id(1)
    kband[0:blk] = kp_ref[...]
    kband[blk:2 * blk] = kc_ref[...]
    kband[2 * blk:] = kn_ref[...]
    vband[0:blk] = vp_ref[...]
    vband[blk:2 * blk] = vc_ref[...]
    vband[2 * blk:] = vn_ref[...]
    band = NA_ROWS * GRID_W
    lane_head = lax.broadcasted_iota(jnp.int32, (GRID_W, MIXER_WIDTH), 1) // HEAD_DIM
    for jr in range(rb):
        r = i * rb + jr
        rs = jnp.clip(r - NA_ROWS // 2, 0, grid_rows - NA_ROWS)
        start = pl.multiple_of((rs - (i - 1) * rb) * GRID_W, GRID_W)
        k = kband[pl.ds(start, band), :]
        v = vband[pl.ds(start, band), :]
        qs = _stack_masked(q_ref[jr * GRID_W:(jr + 1) * GRID_W, :], C_HEADS)
        s = lax.dot_general(qs, k, (((1,), (1,)), ((), ())), preferred_element_type=F32)
        s = s + bias_ref[r - rs].reshape(C_HEADS * GRID_W, band)
        m = jnp.max(s, axis=1, keepdims=True)
        p = jnp.exp2(s - m)
        l = jnp.sum(p, axis=1, keepdims=True)
        pv = jnp.dot(p.astype(BF16), v, preferred_element_type=F32) / l
        o = jnp.zeros((GRID_W, MIXER_WIDTH), F32)
        for hd in range(C_HEADS):
            o = jnp.where(lane_head == hd, pv[hd * GRID_W:(hd + 1) * GRID_W], o)
        o_ref[jr * GRID_W:(jr + 1) * GRID_W, :] = o.astype(o_ref.dtype)


def _nbr_call(q, k, v, bias):
    b, seq, w = q.shape
    grid_rows = seq // GRID_W
    rb = min(C_BLOCK_ROWS, grid_rows)
    blk = rb * GRID_W
    nblk = seq // blk
    cur = pl.BlockSpec((None, blk, w), lambda bi, i: (bi, i, 0))
    prev = pl.BlockSpec((None, blk, w), lambda bi, i: (bi, jnp.maximum(i - 1, 0), 0))
    nxt = pl.BlockSpec((None, blk, w), lambda bi, i: (bi, jnp.minimum(i + 1, nblk - 1), 0))
    return pl.pallas_call(
        functools.partial(_nbr_kernel, grid_rows=grid_rows),
        grid=(b, nblk),
        in_specs=[cur, prev, cur, nxt, prev, cur, nxt,
                  pl.BlockSpec(bias.shape, lambda bi, i: (0, 0, 0, 0))],
        out_specs=cur,
        out_shape=jax.ShapeDtypeStruct((b, seq, w), BF16),
        scratch_shapes=[pltpu.VMEM((3 * blk, w), BF16), pltpu.VMEM((3 * blk, w), BF16)],
        compiler_params=pltpu.CompilerParams(
            dimension_semantics=("parallel", "parallel"), vmem_limit_bytes=VMEM_LIMIT_BYTES),
        name="nbr_attention",
    )(q, k, k, k, v, v, v, bias)


def _nbr_bias_table(rpb):
    qc = jnp.arange(GRID_W)
    kc = jnp.arange(GRID_W)
    cs = jnp.clip(qc - NA_COLS // 2, 0, GRID_W - NA_COLS)
    valid = (kc[None, :] >= cs[:, None]) & (kc[None, :] < cs[:, None] + NA_COLS)
    dc = jnp.clip(kc[None, :] - qc[:, None] + (NA_COLS - 1), 0, 2 * NA_COLS - 2)
    d = jnp.arange(NA_ROWS)
    a = jnp.arange(NA_ROWS)
    dr = a[None, :] - d[:, None] + (NA_ROWS - 1)
    tbl = rpb[:, dr[:, :, None, None], dc[None, None, :, :]]
    tbl = jnp.where(valid[None, None, None], tbl.astype(F32) * LOG2E, NEG_BIG)
    tbl = tbl.transpose(1, 0, 3, 2, 4)
    return tbl.reshape(NA_ROWS, C_HEADS, GRID_W, NA_ROWS * GRID_W)


def _channel_kernel(h_ref, ya_ref, yb_ref, yc_ref, yd_ref, wout_ref, gffn_ref, wg_ref, wu_ref,
                    wd_ref, gple_ref, wpg_ref, p_ref, wpp_ref, gfin_ref, o_ref, *, final):
    w = MIXER_WIDTH
    h = h_ref[...]
    for j, y_ref in enumerate((ya_ref, yb_ref, yc_ref, yd_ref)):
        h = h + jnp.dot(y_ref[...], wout_ref[j * w:(j + 1) * w, :], preferred_element_type=F32)

    hn = _rms(h, gffn_ref[...]).astype(BF16)
    ffn = wg_ref.shape[1]
    acc = jnp.zeros(h.shape, F32)
    for c in range(ffn // FFN_CHUNK):
        cols = slice(c * FFN_CHUNK, (c + 1) * FFN_CHUNK)
        gate = jnp.dot(hn, wg_ref[:, cols], preferred_element_type=F32)
        up = jnp.dot(hn, wu_ref[:, cols], preferred_element_type=F32)
        act = (gate * jax.nn.sigmoid(gate) * up).astype(BF16)
        acc = acc + jnp.dot(act, wd_ref[cols, :], preferred_element_type=F32)
    h = h + acc

    hn = _rms(h, gple_ref[...]).astype(BF16)
    gate = jax.nn.sigmoid(jnp.dot(hn, wpg_ref[...], preferred_element_type=F32))
    emb = jnp.dot(p_ref[...].astype(BF16), wpp_ref[...], preferred_element_type=F32)
    h = h + gate * emb
    if final:
        h = _rms(h, gfin_ref[...])
    o_ref[...] = h


def _channel_call(h, ya, yb, yc, yd, wout, gffn, wg, wu, wd, gple, wpg, p, wpp, gfin, final, seq):
    rows, d = h.shape
    tm = min(ROW_TILE, seq)
    row_spec = lambda w: pl.BlockSpec((tm, w), lambda i: (i, 0))
    resident = lambda a: pl.BlockSpec(a.shape, lambda i: (0,) * a.ndim,
                                      pipeline_mode=pl.Buffered(1))
    return pl.pallas_call(
        functools.partial(_channel_kernel, final=final),
        grid=(rows // tm,),
        in_specs=[row_spec(d)] + [row_spec(MIXER_WIDTH)] * 4
                 + [resident(wout), resident(gffn), resident(wg), resident(wu), resident(wd),
                    resident(gple), resident(wpg), row_spec(p.shape[1]), resident(wpp),
                    resident(gfin)],
        out_specs=row_spec(d),
        out_shape=jax.ShapeDtypeStruct((rows, d), F32),
        compiler_params=pltpu.CompilerParams(
            dimension_semantics=("parallel",), vmem_limit_bytes=VMEM_LIMIT_BYTES),
        name="channel_mixers",
    )(h, ya, yb, yc, yd, wout, gffn, wg, wu, wd, gple, wpg, p, wpp, gfin)


def _rope_tables(seq):
    t = jnp.arange(seq)
    row = (t // GRID_W).astype(F32)
    col = (t % GRID_W).astype(F32)
    n_freq = HEAD_DIM // 4
    inv = ROPE_THETA ** (-jnp.arange(n_freq, dtype=F32) / n_freq)
    ang = jnp.concatenate([row[:, None] * inv, col[:, None] * inv], axis=-1)
    cos, sin = jnp.cos(ang), jnp.sin(ang)
    heads = LANES // HEAD_DIM
    return (jnp.tile(jnp.concatenate([cos, cos], axis=-1), (1, heads)),
            jnp.tile(jnp.concatenate([-sin, sin], axis=-1), (1, heads)))


def kernel(x, p, g_mix, w_in, a_q_norm, a_k_norm, b_lam_q, b_lam_k, b_sub_norm, c_rpb, d_ln_g,
           d_ln_b, d_w_s, d_b_s, w_out, g_ffn, w_gate, w_up, w_down, g_ple, w_ple_gate,
           w_ple_proj, g_final):
    b, seq, d = x.shape
    depth = w_in.shape[0]
    assert seq % GRID_W == 0 and seq // GRID_W >= NA_ROWS and seq % D_CHUNK == 0
    rows = b * seq
    h = x.reshape(rows, d)
    cos_t, sin_t = _rope_tables(seq)
    slopes = (2.0 ** (-8.0 / B_HEADS)) ** jnp.arange(1, B_HEADS + 1, dtype=F32) * LOG2E
    row2 = lambda a: a.reshape(1, -1).astype(F32)
    for i in range(depth):
        bs_tile = jnp.repeat(d_b_s[i].T, HEAD_DIM, axis=1)
        aq, ak, av, bq, bk, bv, cq, ck, cv, yd = _proj_call(
            h, row2(g_mix[i]), w_in[i].astype(BF16),
            row2(jnp.tile(a_q_norm[i], A_HEADS)), row2(jnp.tile(a_k_norm[i], A_KV_HEADS)),
            cos_t, sin_t, row2(d_ln_g[i]), row2(d_ln_b[i]), d_w_s[i].astype(BF16),
            bs_tile.astype(F32), seq)
        sh = lambda a: a.reshape(b, seq, MIXER_WIDTH)
        ya = _gqa_call(sh(aq), sh(ak), sh(av))
        lam_init = 0.8 - 0.6 * math.exp(-0.3 * i)
        scal = jnp.concatenate([jnp.full((1,), lam_init, F32), slopes])
        yb = _diff_call(scal, sh(bq), sh(bk), sh(bv), b_lam_q[i].astype(F32),
                        b_lam_k[i].astype(F32), row2(jnp.tile(b_sub_norm[i], LANES // HEAD_DIM)))
        yc = _nbr_call(sh(cq), sh(ck), sh(cv), _nbr_bias_table(c_rpb[i]))
        fl = lambda a: a.reshape(rows, MIXER_WIDTH)
        h = _channel_call(
            h, fl(ya), fl(yb), fl(yc), yd, w_out[i].astype(BF16), row2(g_ffn[i]),
            w_gate[i].astype(BF16), w_up[i].astype(BF16), w_down[i].astype(BF16),
            row2(g_ple[i]), w_ple_gate[i].astype(BF16), p[i].reshape(rows, -1),
            w_ple_proj[i].astype(BF16), row2(g_final), i == depth - 1, seq)
    return h.reshape(b, seq, d)
```

```python
import functools
import math

import jax
import jax.numpy as jnp
import numpy as np
from jax import lax
from jax.experimental import pallas as pl
from jax.experimental.pallas import tpu as pltpu

F32 = jnp.float32
BF16 = jnp.bfloat16

HEAD_DIM = 64
GRID_W = 64
RMS_EPS = 1e-6
LN_EPS = 1e-5
A_HEADS = 4
A_KV_HEADS = 2
ROPE_THETA = 10000.0
B_HEADS = 4
B_QK_DIM = HEAD_DIM // 2
C_HEADS = 4
NA_ROWS = 8
NA_COLS = 16
D_GROUPS = 4
D_CHUNK = 128
MIXER_WIDTH = 256

LOG2E = math.log2(math.e)
NEG_BIG = -1e30

LANES = 128
VMEM_LIMIT_BYTES = 56 * 1024 * 1024

ROW_TILE = 512
A_Q_TILE = 256
B_Q_TILE = 128
C_BLOCK_ROWS = 8
FFN_CHUNK = 256


def _group_ones(width, group):
    r = lax.broadcasted_iota(jnp.int32, (width, width), 0) // group
    c = lax.broadcasted_iota(jnp.int32, (width, width), 1) // group
    return (r == c).astype(F32)


def _group_sum(x, group):
    return jnp.dot(x, _group_ones(x.shape[-1], group), precision=lax.Precision.HIGHEST,
                   preferred_element_type=F32)


def _rms(x, g):
    ms = jnp.mean(x * x, axis=-1, keepdims=True)
    return x * lax.rsqrt(ms + RMS_EPS) * g


def _proj_kernel(h_ref, gmix_ref, win_ref, gq_ref, gk_ref, cos_ref, sin_ref,
                 lng_ref, lnb_ref, ws_ref, bs_ref,
                 aqt_o, ak_o, avt_o, bqt_o, bk_o, bvt_o, cq_o, ck_o, cv_o, yd_o):
    tm = h_ref.shape[0]
    hn = _rms(h_ref[...], gmix_ref[...]).astype(BF16)

    def proj(lo, width):
        return jnp.dot(hn, win_ref[:, lo:lo + width], preferred_element_type=F32)

    cos = cos_ref[...]
    sin = sin_ref[...]

    def qk_norm_rope(y, gain, cos_t, sin_t):
        w = y.shape[-1]
        ss = _group_sum(y * y, HEAD_DIM)
        y = y * lax.rsqrt(ss * (1.0 / HEAD_DIM) + RMS_EPS) * gain
        lane = lax.broadcasted_iota(jnp.int32, y.shape, 1)
        half = HEAD_DIM // 2
        swapped = jnp.where(lane % HEAD_DIM < half,
                            pltpu.roll(y, w - half, axis=1), pltpu.roll(y, half, axis=1))
        return y * cos_t + swapped * sin_t

    a_scale = HEAD_DIM ** -0.5 * LOG2E
    cos2 = jnp.concatenate([cos, cos], axis=1)
    sin2 = jnp.concatenate([sin, sin], axis=1)
    aq = qk_norm_rope(proj(0, 256), gq_ref[...], cos2, sin2) * a_scale
    aqt_o[...] = aq.T.astype(BF16)
    ak = qk_norm_rope(proj(256, 128), gk_ref[...], cos, sin)
    av = proj(384, 128)
    lane128 = lax.broadcasted_iota(jnp.int32, (tm, LANES), 1)
    first = lane128 < HEAD_DIM

    def dup_heads(y):
        r = pltpu.roll(y, HEAD_DIM, axis=1)
        return jnp.concatenate([jnp.where(first, y, r), jnp.where(first, r, y)], axis=1)

    ak_o[...] = dup_heads(ak).astype(BF16)
    avt_o[...] = dup_heads(av).T.astype(BF16)

    bqt_o[...] = (proj(512, 256) * (B_QK_DIM ** -0.5 * LOG2E)).T.astype(BF16)
    bk_o[...] = proj(768, 256).astype(BF16)
    bvt_o[...] = proj(1024, 256).T.astype(BF16)

    cq_o[...] = (proj(1280, 256) * a_scale).astype(BF16)
    ck_o[...] = proj(1536, 256).astype(BF16)
    cv_o[...] = proj(1792, 256).astype(BF16)

    def gelu(t):
        return 0.5 * t * (1.0 + lax.erf(t * (2.0 ** -0.5)))

    u = gelu(proj(2048, 256))
    vv = gelu(proj(2304, 256))
    mu = jnp.mean(vv, axis=-1, keepdims=True)
    vc = vv - mu
    var = jnp.mean(vc * vc, axis=-1, keepdims=True)
    vv = (vc * lax.rsqrt(var + LN_EPS) * lng_ref[...] + lnb_ref[...]).astype(BF16)
    lane_group = lax.broadcasted_iota(jnp.int32, (D_CHUNK, MIXER_WIDTH), 1) // HEAD_DIM
    for n in range(tm // D_CHUNK):
        rows = slice(n * D_CHUNK, (n + 1) * D_CHUNK)
        vchunk = vv[rows, :]
        sv = bs_ref[...]
        for g in range(D_GROUPS):
            mixed = jnp.dot(ws_ref[g], vchunk, preferred_element_type=F32)
            sv = sv + jnp.where(lane_group == g, mixed, 0.0)
        yd_o[rows, :] = (u[rows, :] * sv).astype(BF16)


def _proj_call(h, gmix, win, gq, gk, cos_t, sin_t, lng, lnb, ws, bs, seq):
    rows, d = h.shape
    tm = min(ROW_TILE, seq)
    steps_per_seq = seq // tm
    row_spec = lambda w: pl.BlockSpec((tm, w), lambda i: (i, 0))
    full = lambda a: pl.BlockSpec(a.shape, lambda i: (0,) * a.ndim)
    tab_spec = pl.BlockSpec((tm, LANES), lambda i: (i % steps_per_seq, 0))
    row_out = (row_spec(MIXER_WIDTH), jax.ShapeDtypeStruct((rows, MIXER_WIDTH), BF16))
    qt_out = (pl.BlockSpec((MIXER_WIDTH, tm), lambda i: (0, i)),
              jax.ShapeDtypeStruct((MIXER_WIDTH, rows), BF16))
    vt_out = (pl.BlockSpec((None, MIXER_WIDTH, tm), lambda i: (i, 0, 0)),
              jax.ShapeDtypeStruct((rows // tm, MIXER_WIDTH, tm), BF16))
    outs = [qt_out, row_out, vt_out, qt_out, row_out, vt_out] + [row_out] * 4
    return pl.pallas_call(
        _proj_kernel,
        grid=(rows // tm,),
        in_specs=[row_spec(d), full(gmix), full(win), full(gq), full(gk), tab_spec, tab_spec,
                  full(lng), full(lnb), full(ws), full(bs)],
        out_specs=[o[0] for o in outs],
        out_shape=[o[1] for o in outs],
        compiler_params=pltpu.CompilerParams(
            dimension_semantics=("parallel",), vmem_limit_bytes=VMEM_LIMIT_BYTES),
        name="proj_mixers",
    )(h, gmix, win, gq, gk, cos_t, sin_t, lng, lnb, ws, bs)


def _online_softmax_step(s, vt, m_scr, l_scr, acc_scr, col_offset=None):
    m_prev = m_scr[...]
    m_cur = jnp.max(s, axis=0, keepdims=True)
    if col_offset is not None:
        m_cur = m_cur - col_offset
    m_next = jnp.maximum(m_prev, m_cur)
    alpha = jnp.exp2(m_prev - m_next)
    shift = m_next if col_offset is None else m_next + col_offset
    p = jnp.exp2(s - shift)
    l_scr[...] = alpha * l_scr[...] + jnp.sum(p, axis=0, keepdims=True)
    acc_scr[...] = alpha * acc_scr[...] + jnp.dot(vt, p.astype(BF16), preferred_element_type=F32)
    m_scr[...] = m_next


def _stack_masked_t(qt, groups):
    qt = qt.astype(F32)
    width = qt.shape[0] // groups
    row = lax.broadcasted_iota(jnp.int32, qt.shape, 0) // width
    zero = jnp.zeros_like(qt)
    return jnp.concatenate([jnp.where(row == j, qt, zero) for j in range(groups)],
                           axis=1).astype(BF16)


def _stack_masked(q, groups):
    width = q.shape[1] // groups
    lane = lax.broadcasted_iota(jnp.int32, q.shape, 1) // width
    zero = jnp.zeros_like(q)
    return jnp.concatenate([jnp.where(lane == j, q, zero) for j in range(groups)], axis=0)


def _init_softmax_state(m_scr, l_scr, acc_scr):
    m_scr[...] = jnp.full(m_scr.shape, NEG_BIG, F32)
    l_scr[...] = jnp.zeros(l_scr.shape, F32)
    acc_scr[...] = jnp.zeros(acc_scr.shape, F32)


def _dense_specs(b, seq, tq, nk, tk):
    nq = seq // tq
    qt_spec = pl.BlockSpec((LANES, tq), lambda bi, g, qi: (g, bi * nq + qi))
    k_spec = pl.BlockSpec((None, seq, LANES), lambda bi, g, qi: (bi, 0, g))
    vt_spec = pl.BlockSpec((None, nk, LANES, tk), lambda bi, g, qi: (bi, 0, g, 0))
    o_spec = pl.BlockSpec((None, tq, LANES), lambda bi, g, qi: (bi, qi, g))
    return qt_spec, k_spec, vt_spec, o_spec


def _pipelined_chunks(nk, scores, consume, s_scr):
    assert nk % 2 == 0
    s_scr[0] = scores(0)

    def body(i, carry):
        c = 2 * i
        s_scr[1] = scores(c + 1)
        consume(c, s_scr[0])
        s_scr[0] = scores(jnp.minimum(c + 2, nk - 1))
        consume(c + 1, s_scr[1])
        return carry

    lax.fori_loop(0, nk // 2, body, 0)


def _gqa_kernel(qt_ref, k_ref, vt_ref, o_ref, qs_scr, s_scr, m_scr, l_scr, acc_scr):
    tq = qt_ref.shape[1]
    nk, _, tk = vt_ref.shape
    qs_scr[...] = _stack_masked_t(qt_ref[...], 2)
    _init_softmax_state(m_scr, l_scr, acc_scr)

    def scores(c):
        start = pl.multiple_of(c * tk, tk)
        return jnp.dot(k_ref[pl.ds(start, tk), :], qs_scr[...], preferred_element_type=F32)

    def consume(c, s):
        _online_softmax_step(s, vt_ref[c], m_scr, l_scr, acc_scr)

    _pipelined_chunks(nk, scores, consume, s_scr)
    o = acc_scr[...] / l_scr[...]
    row = lax.broadcasted_iota(jnp.int32, (LANES, tq), 0)
    o_ref[...] = jnp.where(row < HEAD_DIM, o[:, :tq], o[:, tq:]).T.astype(o_ref.dtype)


def _gqa_call(qt, k, vt):
    b, nk, _, tk = vt.shape
    seq = k.shape[1]
    tq = min(A_Q_TILE, seq)
    qt_spec, k_spec, vt_spec, o_spec = _dense_specs(b, seq, tq, nk, tk)
    return pl.pallas_call(
        _gqa_kernel,
        grid=(b, A_KV_HEADS, seq // tq),
        in_specs=[qt_spec, k_spec, vt_spec],
        out_specs=o_spec,
        out_shape=jax.ShapeDtypeStruct((b, seq, MIXER_WIDTH), BF16),
        scratch_shapes=[pltpu.VMEM((LANES, 2 * tq), BF16),
                        pltpu.VMEM((2, tk, 2 * tq), F32),
                        pltpu.VMEM((1, 2 * tq), F32),
                        pltpu.VMEM((1, 2 * tq), F32),
                        pltpu.VMEM((LANES, 2 * tq), F32)],
        compiler_params=pltpu.CompilerParams(
            dimension_semantics=("parallel", "parallel", "arbitrary"),
            vmem_limit_bytes=VMEM_LIMIT_BYTES),
        name="gqa_attention",
    )(qt, k, vt)


def _diff_kernel(scal_ref, qt_ref, k_ref, vt_ref, lq_ref, lk_ref, gsub_ref, o_ref,
                 qs_scr, s_scr, key_scr, m_scr, l_scr, acc_scr):
    tq = qt_ref.shape[1]
    nk, _, tk = vt_ref.shape
    pair = pl.program_id(1)
    qi = pl.program_id(2)
    lam_init = scal_ref[0]
    slope_a = scal_ref[1 + 2 * pair]
    slope_b = scal_ref[2 + 2 * pair]
    qs_scr[...] = _stack_masked_t(qt_ref[...], 4)
    _init_softmax_state(m_scr, l_scr, acc_scr)

    q0 = qi * tq
    c_mid = q0 // tk
    col = lax.broadcasted_iota(jnp.int32, (1, 4 * tq), 1)
    col_slope = jnp.where(col < 2 * tq, slope_a, slope_b)
    col_pos = (q0 + col % tq).astype(F32)
    key_off = lax.broadcasted_iota(jnp.int32, (tk, 4 * tq), 0)

    @pl.when(qi == 0)
    def _():
        key_table = key_off.astype(F32) * col_slope
        key_scr[0] = key_table
        key_scr[1] = -key_table

    rel = lax.broadcasted_iota(jnp.int32, (tk, 4 * tq), 1) % tq - key_off + (q0 - c_mid * tk)
    key_scr[2] = -(jnp.abs(rel).astype(F32) * col_slope)

    def scores(c):
        start = pl.multiple_of(c * tk, tk)
        s = jnp.dot(k_ref[pl.ds(start, tk), :], qs_scr[...], preferred_element_type=F32)
        return s + key_scr[jnp.where(c == c_mid, 2, jnp.where(c < c_mid, 0, 1))]

    def consume(c, s):
        sign = jnp.where(c == c_mid, 0.0, jnp.where(c < c_mid, 1.0, -1.0)).astype(F32)
        col_offset = col_slope * (sign * (col_pos - (c * tk).astype(F32)))
        _online_softmax_step(s, vt_ref[c], m_scr, l_scr, acc_scr, col_offset)

    _pipelined_chunks(nk, scores, consume, s_scr)

    t = jnp.exp(jnp.sum(lq_ref[...] * lk_ref[...], axis=1, keepdims=True))
    lam = t[0:1, :] - t[1:2, :] + lam_init
    o = acc_scr[...] / l_scr[...]
    o_a = o[:, 0:tq] - lam * o[:, tq:2 * tq]
    o_b = o[:, 2 * tq:3 * tq] - lam * o[:, 3 * tq:]
    row = lax.broadcasted_iota(jnp.int32, (LANES, tq), 0)
    o = jnp.where(row < HEAD_DIM, o_a, o_b).T
    ss = _group_sum(o * o, HEAD_DIM)
    o = o * lax.rsqrt(ss * (1.0 / HEAD_DIM) + RMS_EPS) * gsub_ref[...] * (1.0 - lam_init)
    o_ref[...] = o.astype(o_ref.dtype)


def _diff_call(scal, qt, k, vt, lq, lk, gsub):
    b, nk, _, tk = vt.shape
    seq = k.shape[1]
    tq = min(B_Q_TILE, seq)
    assert tk % tq == 0
    qt_spec, k_spec, vt_spec, o_spec = _dense_specs(b, seq, tq, nk, tk)
    full = lambda a: pl.BlockSpec(a.shape, lambda bi, g, qi: (0,) * a.ndim)
    return pl.pallas_call(
        _diff_kernel,
        grid=(b, B_HEADS // 2, seq // tq),
        in_specs=[pl.BlockSpec(memory_space=pltpu.SMEM), qt_spec, k_spec, vt_spec,
                  full(lq), full(lk), full(gsub)],
        out_specs=o_spec,
        out_shape=jax.ShapeDtypeStruct((b, seq, MIXER_WIDTH), BF16),
        scratch_shapes=[pltpu.VMEM((LANES, 4 * tq), BF16),
                        pltpu.VMEM((2, tk, 4 * tq), F32),
                        pltpu.VMEM((3, tk, 4 * tq), F32),
                        pltpu.VMEM((1, 4 * tq), F32),
                        pltpu.VMEM((1, 4 * tq), F32),
                        pltpu.VMEM((LANES, 4 * tq), F32)],
        compiler_params=pltpu.CompilerParams(
            dimension_semantics=("parallel", "parallel", "arbitrary"),
            vmem_limit_bytes=VMEM_LIMIT_BYTES),
        name="diff_attention",
    )(scal, qt, k, vt, lq, lk, gsub)


def _nbr_kernel(q_ref, kp_ref, kc_ref, kn_ref, vp_ref, vc_ref, vn_ref, bias_ref, o_ref,
                kband, vband, *, grid_rows):
    blk = q_ref.shape[0]
    rb = blk // GRID_W
    i = pl.program_id(1)
    kband[0:blk] = kp_ref[...]
    kband[blk:2 * blk] = kc_ref[...]
    kband[2 * blk:] = kn_ref[...]
    vband[0:blk] = vp_ref[...]
    vband[blk:2 * blk] = vc_ref[...]
    vband[2 * blk:] = vn_ref[...]
    band = NA_ROWS * GRID_W
    lane_head = lax.broadcasted_iota(jnp.int32, (GRID_W, MIXER_WIDTH), 1) // HEAD_DIM
    for jr in range(rb):
        r = i * rb + jr
        rs = jnp.clip(r - NA_ROWS // 2, 0, grid_rows - NA_ROWS)
        start = pl.multiple_of((rs - (i - 1) * rb) * GRID_W, GRID_W)
        k = kband[pl.ds(start, band), :]
        v = vband[pl.ds(start, band), :]
        qs = _stack_masked(q_ref[jr * GRID_W:(jr + 1) * GRID_W, :], C_HEADS)
        s = lax.dot_general(qs, k, (((1,), (1,)), ((), ())), preferred_element_type=F32)
        s = s + bias_ref[r - rs].reshape(C_HEADS * GRID_W, band)
        m = jnp.max(s, axis=1, keepdims=True)
        p = jnp.exp2(s - m)
        l = jnp.sum(p, axis=1, keepdims=True)
        pv = jnp.dot(p.astype(BF16), v, preferred_element_type=F32) / l
        o = jnp.zeros((GRID_W, MIXER_WIDTH), F32)
        for hd in range(C_HEADS):
            o = jnp.where(lane_head == hd, pv[hd * GRID_W:(hd + 1) * GRID_W], o)
        o_ref[jr * GRID_W:(jr + 1) * GRID_W, :] = o.astype(o_ref.dtype)


def _nbr_call(q, k, v, bias):
    b, seq, w = q.shape
    grid_rows = seq // GRID_W
    rb = min(C_BLOCK_ROWS, grid_rows)
    blk = rb * GRID_W
    nblk = seq // blk
    cur = pl.BlockSpec((None, blk, w), lambda bi, i: (bi, i, 0))
    prev = pl.BlockSpec((None, blk, w), lambda bi, i: (bi, jnp.maximum(i - 1, 0), 0))
    nxt = pl.BlockSpec((None, blk, w), lambda bi, i: (bi, jnp.minimum(i + 1, nblk - 1), 0))
    return pl.pallas_call(
        functools.partial(_nbr_kernel, grid_rows=grid_rows),
        grid=(b, nblk),
        in_specs=[cur, prev, cur, nxt, prev, cur, nxt,
                  pl.BlockSpec(bias.shape, lambda bi, i: (0, 0, 0, 0))],
        out_specs=cur,
        out_shape=jax.ShapeDtypeStruct((b, seq, w), BF16),
        scratch_shapes=[pltpu.VMEM((3 * blk, w), BF16), pltpu.VMEM((3 * blk, w), BF16)],
        compiler_params=pltpu.CompilerParams(
            dimension_semantics=("parallel", "parallel"), vmem_limit_bytes=VMEM_LIMIT_BYTES),
        name="nbr_attention",
    )(q, k, k, k, v, v, v, bias)


def _nbr_bias_table(rpb):
    qc = np.arange(GRID_W)
    kc = np.arange(GRID_W)
    cs = np.clip(qc - NA_COLS // 2, 0, GRID_W - NA_COLS)
    valid = (kc[None, :] >= cs[:, None]) & (kc[None, :] < cs[:, None] + NA_COLS)
    dc = kc[None, :] - qc[:, None] + (NA_COLS - 1)
    d = np.arange(NA_ROWS)
    a = np.arange(NA_ROWS)
    dr = a[None, :] - d[:, None] + (NA_ROWS - 1)
    pick_r = (dr[:, :, None] == np.arange(2 * NA_ROWS - 1)).astype(np.float32)
    pick_c = ((dc[:, :, None] == np.arange(2 * NA_COLS - 1)) & valid[:, :, None]).astype(np.float32)
    tbl = jnp.einsum("hrc,dar,qkc->dhqak", rpb.astype(F32) * LOG2E, pick_r, pick_c,
                     precision=lax.Precision.HIGHEST)
    tbl = jnp.where(valid[None, None, :, None, :], tbl, NEG_BIG)
    return tbl.reshape(NA_ROWS, C_HEADS, GRID_W, NA_ROWS * GRID_W)


def _channel_kernel(h_ref, ya_ref, yb_ref, yc_ref, yd_ref, wout_ref, gffn_ref, wg_ref, wu_ref,
                    wd_ref, gple_ref, wpg_ref, p_ref, wpp_ref, gfin_ref, o_ref, *, final):
    w = MIXER_WIDTH
    h = h_ref[...]
    for j, y_ref in enumerate((ya_ref, yb_ref, yc_ref, yd_ref)):
        h = h + jnp.dot(y_ref[...], wout_ref[j * w:(j + 1) * w, :], preferred_element_type=F32)

    hn = _rms(h, gffn_ref[...]).astype(BF16)
    ffn = wg_ref.shape[1]
    acc = jnp.zeros(h.shape, F32)
    for c in range(ffn // FFN_CHUNK):
        cols = slice(c * FFN_CHUNK, (c + 1) * FFN_CHUNK)
        gate = jnp.dot(hn, wg_ref[:, cols], preferred_element_type=F32)
        up = jnp.dot(hn, wu_ref[:, cols], preferred_element_type=F32)
        act = (gate * jax.nn.sigmoid(gate) * up).astype(BF16)
        acc = acc + jnp.dot(act, wd_ref[cols, :], preferred_element_type=F32)
    h = h + acc

    hn = _rms(h, gple_ref[...]).astype(BF16)
    gate = jax.nn.sigmoid(jnp.dot(hn, wpg_ref[...], preferred_element_type=F32))
    emb = jnp.dot(p_ref[...].astype(BF16), wpp_ref[...], preferred_element_type=F32)
    h = h + gate * emb
    if final:
        h = _rms(h, gfin_ref[...])
    o_ref[...] = h


def _channel_call(h, ya, yb, yc, yd, wout, gffn, wg, wu, wd, gple, wpg, p, wpp, gfin, final, seq):
    rows, d = h.shape
    tm = min(ROW_TILE, seq)
    row_spec = lambda w: pl.BlockSpec((tm, w), lambda i: (i, 0))
    resident = lambda a: pl.BlockSpec(a.shape, lambda i: (0,) * a.ndim,
                                      pipeline_mode=pl.Buffered(1))
    return pl.pallas_call(
        functools.partial(_channel_kernel, final=final),
        grid=(rows // tm,),
        in_specs=[row_spec(d)] + [row_spec(MIXER_WIDTH)] * 4
                 + [resident(wout), resident(gffn), resident(wg), resident(wu), resident(wd),
                    resident(gple), resident(wpg), row_spec(p.shape[1]), resident(wpp),
                    resident(gfin)],
        out_specs=row_spec(d),
        out_shape=jax.ShapeDtypeStruct((rows, d), F32),
        compiler_params=pltpu.CompilerParams(
            dimension_semantics=("parallel",), vmem_limit_bytes=VMEM_LIMIT_BYTES),
        name="channel_mixers",
    )(h, ya, yb, yc, yd, wout, gffn, wg, wu, wd, gple, wpg, p, wpp, gfin)


def _rope_tables(seq):
    t = jnp.arange(seq)
    row = (t // GRID_W).astype(F32)
    col = (t % GRID_W).astype(F32)
    n_freq = HEAD_DIM // 4
    inv = ROPE_THETA ** (-jnp.arange(n_freq, dtype=F32) / n_freq)
    ang = jnp.concatenate([row[:, None] * inv, col[:, None] * inv], axis=-1)
    cos, sin = jnp.cos(ang), jnp.sin(ang)
    heads = LANES // HEAD_DIM
    return (jnp.tile(jnp.concatenate([cos, cos], axis=-1), (1, heads)),
            jnp.tile(jnp.concatenate([-sin, sin], axis=-1), (1, heads)))


def kernel(x, p, g_mix, w_in, a_q_norm, a_k_norm, b_lam_q, b_lam_k, b_sub_norm, c_rpb, d_ln_g,
           d_ln_b, d_w_s, d_b_s, w_out, g_ffn, w_gate, w_up, w_down, g_ple, w_ple_gate,
           w_ple_proj, g_final):
    b, seq, d = x.shape
    depth = w_in.shape[0]
    assert seq % GRID_W == 0 and seq // GRID_W >= NA_ROWS and seq % D_CHUNK == 0
    rows = b * seq
    h = x.reshape(rows, d)
    cos_t, sin_t = _rope_tables(seq)
    slopes = (2.0 ** (-8.0 / B_HEADS)) ** jnp.arange(1, B_HEADS + 1, dtype=F32) * LOG2E
    row2 = lambda a: a.reshape(1, -1).astype(F32)
    for i in range(depth):
        bs_tile = jnp.repeat(d_b_s[i].T, HEAD_DIM, axis=1)
        aqt, ak, avt, bqt, bk, bvt, cq, ck, cv, yd = _proj_call(
            h, row2(g_mix[i]), w_in[i].astype(BF16),
            row2(jnp.tile(a_q_norm[i], A_HEADS)), row2(jnp.tile(a_k_norm[i], A_KV_HEADS)),
            cos_t, sin_t, row2(d_ln_g[i]), row2(d_ln_b[i]), d_w_s[i].astype(BF16),
            bs_tile.astype(F32), seq)
        sh = lambda a: a.reshape(b, seq, MIXER_WIDTH)
        vt4 = lambda a: a.reshape(b, -1, MIXER_WIDTH, a.shape[-1])
        ya = _gqa_call(aqt, sh(ak), vt4(avt))
        lam_init = 0.8 - 0.6 * math.exp(-0.3 * i)
        scal = jnp.concatenate([jnp.full((1,), lam_init, F32), slopes])
        yb = _diff_call(scal, bqt, sh(bk), vt4(bvt), b_lam_q[i].astype(F32),
                        b_lam_k[i].astype(F32), row2(jnp.tile(b_sub_norm[i], LANES // HEAD_DIM)))
        yc = _nbr_call(sh(cq), sh(ck), sh(cv), _nbr_bias_table(c_rpb[i]))
        fl = lambda a: a.reshape(rows, MIXER_WIDTH)
        h = _channel_call(
            h, fl(ya), fl(yb), fl(yc), yd, w_out[i].astype(BF16), row2(g_ffn[i]),
            w_gate[i].astype(BF16), w_up[i].astype(BF16), w_down[i].astype(BF16),
            row2(g_ple[i]), w_ple_gate[i].astype(BF16), p[i].reshape(rows, -1),
            w_ple_proj[i].astype(BF16), row2(g_final), i == depth - 1, seq)
    return h.reshape(b, seq, d)
```

```python
import functools
import math

import jax
import jax.numpy as jnp
import numpy as np
from jax import lax
from jax.experimental import pallas as pl
from jax.experimental.pallas import tpu as pltpu

F32 = jnp.float32
BF16 = jnp.bfloat16

HEAD_DIM = 64
GRID_W = 64
RMS_EPS = 1e-6
LN_EPS = 1e-5
A_HEADS = 4
A_KV_HEADS = 2
ROPE_THETA = 10000.0
B_HEADS = 4
B_QK_DIM = HEAD_DIM // 2
C_HEADS = 4
NA_ROWS = 8
NA_COLS = 16
D_GROUPS = 4
D_CHUNK = 128
MIXER_WIDTH = 256

LOG2E = math.log2(math.e)
NEG_BIG = -1e30

LANES = 128
BF16_SUBLANES = 16
VMEM_LIMIT_BYTES = 56 * 1024 * 1024

VT_ROWS = HEAD_DIM + BF16_SUBLANES

ROW_TILE = 512
A_Q_TILE = 256
B_Q_TILE = 128
C_BLOCK_ROWS = 8
FFN_CHUNK = 256
PIPELINE_UNROLL = 8


def _group_ones(width, group):
    r = lax.broadcasted_iota(jnp.int32, (width, width), 0) // group
    c = lax.broadcasted_iota(jnp.int32, (width, width), 1) // group
    return (r == c).astype(F32)


def _group_sum(x, group):
    return jnp.dot(x, _group_ones(x.shape[-1], group), precision=lax.Precision.HIGHEST,
                   preferred_element_type=F32)


def _rms(x, g):
    ms = jnp.mean(x * x, axis=-1, keepdims=True)
    return x * lax.rsqrt(ms + RMS_EPS) * g


def _proj_kernel(h_ref, gmix_ref, win_ref, gq_ref, gk_ref, cos_ref, sin_ref,
                 lng_ref, lnb_ref, ws_ref, bs_ref,
                 aqt_o, ak_o, avt_o, bqt_o, bk_o, bvt_o, cq_o, ck_o, cv_o, yd_o):
    tm = h_ref.shape[0]
    hn = _rms(h_ref[...], gmix_ref[...]).astype(BF16)

    def proj(lo, width):
        return jnp.dot(hn, win_ref[:, lo:lo + width], preferred_element_type=F32)

    cos = cos_ref[...]
    sin = sin_ref[...]

    def qk_norm_rope(y, gain, cos_t, sin_t):
        w = y.shape[-1]
        ss = _group_sum(y * y, HEAD_DIM)
        y = y * lax.rsqrt(ss * (1.0 / HEAD_DIM) + RMS_EPS) * gain
        lane = lax.broadcasted_iota(jnp.int32, y.shape, 1)
        half = HEAD_DIM // 2
        swapped = jnp.where(lane % HEAD_DIM < half,
                            pltpu.roll(y, w - half, axis=1), pltpu.roll(y, half, axis=1))
        return y * cos_t + swapped * sin_t

    a_scale = HEAD_DIM ** -0.5 * LOG2E
    cos2 = jnp.concatenate([cos, cos], axis=1)
    sin2 = jnp.concatenate([sin, sin], axis=1)
    aq = qk_norm_rope(proj(0, 256), gq_ref[...], cos2, sin2) * a_scale
    aqt_o[...] = aq.T.astype(BF16)
    ak = qk_norm_rope(proj(256, 128), gk_ref[...], cos, sin)
    av = proj(384, 128)
    lane128 = lax.broadcasted_iota(jnp.int32, (tm, LANES), 1)
    first = lane128 < HEAD_DIM

    def dup_heads(y):
        r = pltpu.roll(y, HEAD_DIM, axis=1)
        return jnp.concatenate([jnp.where(first, y, r), jnp.where(first, r, y)], axis=1)

    ak_o[...] = dup_heads(ak).astype(BF16)
    ones = jnp.ones((BF16_SUBLANES, tm), F32)
    avt = av.T
    avt_o[...] = jnp.concatenate(
        [avt[:HEAD_DIM], ones, avt[HEAD_DIM:], ones], axis=0).astype(BF16)

    bqt_o[...] = (proj(512, 256) * (B_QK_DIM ** -0.5 * LOG2E)).T.astype(BF16)
    bk_o[...] = proj(768, 256).astype(BF16)
    bvt = proj(1024, 256).T
    bvt_o[...] = jnp.concatenate(
        [x for hd in range(B_HEADS) for x in (bvt[hd * HEAD_DIM:(hd + 1) * HEAD_DIM], ones)],
        axis=0).astype(BF16)

    cq_o[...] = (proj(1280, 256) * a_scale).astype(BF16)
    ck_o[...] = proj(1536, 256).astype(BF16)
    cv_o[...] = proj(1792, 256).astype(BF16)

    def gelu(t):
        return 0.5 * t * (1.0 + lax.erf(t * (2.0 ** -0.5)))

    u = gelu(proj(2048, 256))
    vv = gelu(proj(2304, 256))
    mu = jnp.mean(vv, axis=-1, keepdims=True)
    vc = vv - mu
    var = jnp.mean(vc * vc, axis=-1, keepdims=True)
    vv = (vc * lax.rsqrt(var + LN_EPS) * lng_ref[...] + lnb_ref[...]).astype(BF16)
    lane_group = lax.broadcasted_iota(jnp.int32, (D_CHUNK, MIXER_WIDTH), 1) // HEAD_DIM
    for n in range(tm // D_CHUNK):
        rows = slice(n * D_CHUNK, (n + 1) * D_CHUNK)
        vchunk = vv[rows, :]
        sv = bs_ref[...]
        for g in range(D_GROUPS):
            mixed = jnp.dot(ws_ref[g], vchunk, preferred_element_type=F32)
            sv = sv + jnp.where(lane_group == g, mixed, 0.0)
        yd_o[rows, :] = (u[rows, :] * sv).astype(BF16)


def _proj_call(h, gmix, win, gq, gk, cos_t, sin_t, lng, lnb, ws, bs, seq):
    rows, d = h.shape
    tm = min(ROW_TILE, seq)
    steps_per_seq = seq // tm
    row_spec = lambda w: pl.BlockSpec((tm, w), lambda i: (i, 0))
    full = lambda a: pl.BlockSpec(a.shape, lambda i: (0,) * a.ndim)
    tab_spec = pl.BlockSpec((tm, LANES), lambda i: (i % steps_per_seq, 0))
    row_out = (row_spec(MIXER_WIDTH), jax.ShapeDtypeStruct((rows, MIXER_WIDTH), BF16))
    qt_out = (pl.BlockSpec((MIXER_WIDTH, tm), lambda i: (0, i)),
              jax.ShapeDtypeStruct((MIXER_WIDTH, rows), BF16))
    def vt_out(heads):
        r = heads * VT_ROWS
        return (pl.BlockSpec((None, r, tm), lambda i: (i, 0, 0)),
                jax.ShapeDtypeStruct((rows // tm, r, tm), BF16))
    outs = ([qt_out, row_out, vt_out(A_KV_HEADS), qt_out, row_out, vt_out(B_HEADS)]
            + [row_out] * 4)
    return pl.pallas_call(
        _proj_kernel,
        grid=(rows // tm,),
        in_specs=[row_spec(d), full(gmix), full(win), full(gq), full(gk), tab_spec, tab_spec,
                  full(lng), full(lnb), full(ws), full(bs)],
        out_specs=[o[0] for o in outs],
        out_shape=[o[1] for o in outs],
        compiler_params=pltpu.CompilerParams(
            dimension_semantics=("parallel",), vmem_limit_bytes=VMEM_LIMIT_BYTES),
        name="proj_mixers",
    )(h, gmix, win, gq, gk, cos_t, sin_t, lng, lnb, ws, bs)


def _online_softmax_step(s, m_cur, vts, m_scr, acc_scr, col_offset=None):
    m_prev = m_scr[...]
    if col_offset is not None:
        m_cur = m_cur - col_offset
    m_next = jnp.maximum(m_prev, m_cur)
    alpha = jnp.exp2(m_prev - m_next)
    shift = m_next if col_offset is None else m_next + col_offset
    p = jnp.exp2(s - shift).astype(BF16)
    width = p.shape[1] // len(vts)
    pv = [jnp.dot(vt, p[:, j * width:(j + 1) * width], preferred_element_type=F32)
          for j, vt in enumerate(vts)]
    acc_scr[...] = alpha * acc_scr[...] + jnp.concatenate(pv, axis=1)
    m_scr[...] = m_next


def _stack_masked_t(qt, groups):
    qt = qt.astype(F32)
    width = qt.shape[0] // groups
    row = lax.broadcasted_iota(jnp.int32, qt.shape, 0) // width
    zero = jnp.zeros_like(qt)
    return jnp.concatenate([jnp.where(row == j, qt, zero) for j in range(groups)],
                           axis=1).astype(BF16)


def _stack_masked(q, groups):
    width = q.shape[1] // groups
    lane = lax.broadcasted_iota(jnp.int32, q.shape, 1) // width
    zero = jnp.zeros_like(q)
    return jnp.concatenate([jnp.where(lane == j, q, zero) for j in range(groups)], axis=0)


def _init_softmax_state(m_scr, acc_scr):
    m_scr[...] = jnp.full(m_scr.shape, NEG_BIG, F32)
    acc_scr[...] = jnp.zeros(acc_scr.shape, F32)


def _normalized(acc_scr):
    return acc_scr[:HEAD_DIM, :] / acc_scr[HEAD_DIM:HEAD_DIM + 1, :]


def _dense_specs(b, seq, tq, nk, vt_rows, tk):
    nq = seq // tq
    qt_spec = pl.BlockSpec((LANES, tq), lambda bi, g, qi: (g, bi * nq + qi))
    k_spec = pl.BlockSpec((None, seq, LANES), lambda bi, g, qi: (bi, 0, g))
    vt_spec = pl.BlockSpec((None, nk, vt_rows, tk), lambda bi, g, qi: (bi, 0, g, 0))
    o_spec = pl.BlockSpec((None, tq, LANES), lambda bi, g, qi: (bi, qi, g))
    return qt_spec, k_spec, vt_spec, o_spec


def _pipelined_chunks(nk, scores, consume, s_scr, max_scr):
    unroll = PIPELINE_UNROLL if nk % PIPELINE_UNROLL == 0 else 2
    assert nk % unroll == 0 and unroll % 2 == 0

    def produce(c, slot):
        s = scores(c)
        s_scr[slot] = s
        max_scr[slot] = jnp.max(s, axis=0, keepdims=True)

    produce(0, 0)

    def body(i, carry):
        for j in range(unroll):
            c = unroll * i + j
            produce(jnp.minimum(c + 1, nk - 1), (j + 1) % 2)
            consume(c, s_scr[j % 2], max_scr[j % 2])
        return carry

    lax.fori_loop(0, nk // unroll, body, 0)


def _gqa_kernel(qt_ref, k_ref, vt_ref, o_ref, qs_scr, s_scr, max_scr, m_scr, acc_scr):
    tq = qt_ref.shape[1]
    nk, _, tk = vt_ref.shape
    qs_scr[...] = _stack_masked_t(qt_ref[...], 2)
    _init_softmax_state(m_scr, acc_scr)

    def scores(c):
        start = pl.multiple_of(c * tk, tk)
        return jnp.dot(k_ref[pl.ds(start, tk), :], qs_scr[...], preferred_element_type=F32)

    def consume(c, s, m_cur):
        _online_softmax_step(s, m_cur, [vt_ref[c]], m_scr, acc_scr)

    _pipelined_chunks(nk, scores, consume, s_scr, max_scr)
    o = _normalized(acc_scr)
    o_ref[...] = jnp.concatenate([o[:, :tq], o[:, tq:]], axis=0).T.astype(o_ref.dtype)


def _gqa_call(qt, k, vt):
    b, nk, vt_rows, tk = vt.shape
    vt_rows //= A_KV_HEADS
    seq = k.shape[1]
    tq = min(A_Q_TILE, seq)
    qt_spec, k_spec, vt_spec, o_spec = _dense_specs(b, seq, tq, nk, vt_rows, tk)
    return pl.pallas_call(
        _gqa_kernel,
        grid=(b, A_KV_HEADS, seq // tq),
        in_specs=[qt_spec, k_spec, vt_spec],
        out_specs=o_spec,
        out_shape=jax.ShapeDtypeStruct((b, seq, MIXER_WIDTH), BF16),
        scratch_shapes=[pltpu.VMEM((LANES, 2 * tq), BF16),
                        pltpu.VMEM((2, tk, 2 * tq), F32),
                        pltpu.VMEM((2, 1, 2 * tq), F32),
                        pltpu.VMEM((1, 2 * tq), F32),
                        pltpu.VMEM((VT_ROWS, 2 * tq), F32)],
        compiler_params=pltpu.CompilerParams(
            dimension_semantics=("parallel", "parallel", "arbitrary"),
            vmem_limit_bytes=VMEM_LIMIT_BYTES),
        name="gqa_attention",
    )(qt, k, vt)


def _diff_kernel(scal_ref, qt_ref, k_ref, vt_ref, lq_ref, lk_ref, gsub_ref, o_ref,
                 qs_scr, s_scr, max_scr, key_scr, m_scr, acc_scr):
    tq = qt_ref.shape[1]
    nk, _, tk = vt_ref.shape
    pair = pl.program_id(1)
    qi = pl.program_id(2)
    lam_init = scal_ref[0]
    slope_a = scal_ref[1 + 2 * pair]
    slope_b = scal_ref[2 + 2 * pair]
    qs_scr[...] = _stack_masked_t(qt_ref[...], 4)
    _init_softmax_state(m_scr, acc_scr)

    q0 = qi * tq
    c_mid = q0 // tk
    col = lax.broadcasted_iota(jnp.int32, (1, 4 * tq), 1)
    col_slope = jnp.where(col < 2 * tq, slope_a, slope_b)
    col_pos = (q0 + col % tq).astype(F32)
    key_off = lax.broadcasted_iota(jnp.int32, (tk, 4 * tq), 0)

    tiles_per_chunk = tk // tq

    @pl.when(qi == 0)
    def _():
        key_table = key_off.astype(F32) * col_slope
        key_scr[0] = key_table
        key_scr[1] = -key_table
        rel = lax.broadcasted_iota(jnp.int32, (tk, 4 * tq), 1) % tq - key_off
        for t in range(tiles_per_chunk):
            key_scr[2 + t] = -(jnp.abs(rel + t * tq).astype(F32) * col_slope)

    mid_table = 2 + qi % tiles_per_chunk

    def scores(c):
        start = pl.multiple_of(c * tk, tk)
        s = jnp.dot(k_ref[pl.ds(start, tk), :], qs_scr[...], preferred_element_type=F32)
        return s + key_scr[jnp.where(c == c_mid, mid_table, jnp.where(c < c_mid, 0, 1))]

    def consume(c, s, m_cur):
        sign = jnp.where(c == c_mid, 0.0, jnp.where(c < c_mid, 1.0, -1.0)).astype(F32)
        col_offset = col_slope * (sign * (col_pos - (c * tk).astype(F32)))
        vts = [vt_ref[c, :VT_ROWS, :], vt_ref[c, VT_ROWS:, :]]
        _online_softmax_step(s, m_cur, vts, m_scr, acc_scr, col_offset)

    _pipelined_chunks(nk, scores, consume, s_scr, max_scr)

    t = jnp.exp(jnp.sum(lq_ref[...] * lk_ref[...], axis=1, keepdims=True))
    lam = t[0:1, :] - t[1:2, :] + lam_init
    o = _normalized(acc_scr)
    o_a = o[:, 0:tq] - lam * o[:, tq:2 * tq]
    o_b = o[:, 2 * tq:3 * tq] - lam * o[:, 3 * tq:]
    o = jnp.concatenate([o_a, o_b], axis=0).T
    ss = _group_sum(o * o, HEAD_DIM)
    o = o * lax.rsqrt(ss * (1.0 / HEAD_DIM) + RMS_EPS) * gsub_ref[...] * (1.0 - lam_init)
    o_ref[...] = o.astype(o_ref.dtype)


def _diff_call(scal, qt, k, vt, lq, lk, gsub):
    b, nk, vt_rows, tk = vt.shape
    vt_rows //= B_HEADS // 2
    seq = k.shape[1]
    tq = min(B_Q_TILE, seq)
    assert tk % tq == 0
    qt_spec, k_spec, vt_spec, o_spec = _dense_specs(b, seq, tq, nk, vt_rows, tk)
    full = lambda a: pl.BlockSpec(a.shape, lambda bi, g, qi: (0,) * a.ndim)
    return pl.pallas_call(
        _diff_kernel,
        grid=(b, B_HEADS // 2, seq // tq),
        in_specs=[pl.BlockSpec(memory_space=pltpu.SMEM), qt_spec, k_spec, vt_spec,
                  full(lq), full(lk), full(gsub)],
        out_specs=o_spec,
        out_shape=jax.ShapeDtypeStruct((b, seq, MIXER_WIDTH), BF16),
        scratch_shapes=[pltpu.VMEM((LANES, 4 * tq), BF16),
                        pltpu.VMEM((2, tk, 4 * tq), F32),
                        pltpu.VMEM((2, 1, 4 * tq), F32),
                        pltpu.VMEM((2 + tk // tq, tk, 4 * tq), F32),
                        pltpu.VMEM((1, 4 * tq), F32),
                        pltpu.VMEM((VT_ROWS, 4 * tq), F32)],
        compiler_params=pltpu.CompilerParams(
            dimension_semantics=("parallel", "parallel", "arbitrary"),
            vmem_limit_bytes=VMEM_LIMIT_BYTES),
        name="diff_attention",
    )(scal, qt, k, vt, lq, lk, gsub)


def _nbr_kernel(q_ref, kp_ref, kc_ref, kn_ref, vp_ref, vc_ref, vn_ref, bias_ref, o_ref,
                kband, vband, *, grid_rows):
    blk = q_ref.shape[0]
    rb = blk // GRID_W
    i = pl.program_id(1)
    kband[0:blk] = kp_ref[...]
    kband[blk:2 * blk] = kc_ref[...]
    kband[2 * blk:] = kn_ref[...]
    vband[0:blk] = vp_ref[...]
    vband[blk:2 * blk] = vc_ref[...]
    vband[2 * blk:] = vn_ref[...]
    band = NA_ROWS * GRID_W
    lane_head = lax.broadcasted_iota(jnp.int32, (GRID_W, MIXER_WIDTH), 1) // HEAD_DIM
    for jr in range(rb):
        r = i * rb + jr
        rs = jnp.clip(r - NA_ROWS // 2, 0, grid_rows - NA_ROWS)
        start = pl.multiple_of((rs - (i - 1) * rb) * GRID_W, GRID_W)
        k = kband[pl.ds(start, band), :]
        v = vband[pl.ds(start, band), :]
        qs = _stack_masked(q_ref[jr * GRID_W:(jr + 1) * GRID_W, :], C_HEADS)
        s = lax.dot_general(qs, k, (((1,), (1,)), ((), ())), preferred_element_type=F32)
        s = s + bias_ref[r - rs].reshape(C_HEADS * GRID_W, band)
        m = jnp.max(s, axis=1, keepdims=True)
        p = jnp.exp2(s - m)
        l = jnp.sum(p, axis=1, keepdims=True)
        pv = jnp.dot(p.astype(BF16), v, preferred_element_type=F32) / l
        o = jnp.zeros((GRID_W, MIXER_WIDTH), F32)
        for hd in range(C_HEADS):
            o = jnp.where(lane_head == hd, pv[hd * GRID_W:(hd + 1) * GRID_W], o)
        o_ref[jr * GRID_W:(jr + 1) * GRID_W, :] = o.astype(o_ref.dtype)


def _nbr_call(q, k, v, bias):
    b, seq, w = q.shape
    grid_rows = seq // GRID_W
    rb = min(C_BLOCK_ROWS, grid_rows)
    blk = rb * GRID_W
    nblk = seq // blk
    cur = pl.BlockSpec((None, blk, w), lambda bi, i: (bi, i, 0))
    prev = pl.BlockSpec((None, blk, w), lambda bi, i: (bi, jnp.maximum(i - 1, 0), 0))
    nxt = pl.BlockSpec((None, blk, w), lambda bi, i: (bi, jnp.minimum(i + 1, nblk - 1), 0))
    return pl.pallas_call(
        functools.partial(_nbr_kernel, grid_rows=grid_rows),
        grid=(b, nblk),
        in_specs=[cur, prev, cur, nxt, prev, cur, nxt,
                  pl.BlockSpec(bias.shape, lambda bi, i: (0, 0, 0, 0))],
        out_specs=cur,
        out_shape=jax.ShapeDtypeStruct((b, seq, w), BF16),
        scratch_shapes=[pltpu.VMEM((3 * blk, w), BF16), pltpu.VMEM((3 * blk, w), BF16)],
        compiler_params=pltpu.CompilerParams(
            dimension_semantics=("parallel", "parallel"), vmem_limit_bytes=VMEM_LIMIT_BYTES),
        name="nbr_attention",
    )(q, k, k, k, v, v, v, bias)


def _nbr_bias_table(rpb):
    qc = np.arange(GRID_W)
    kc = np.arange(GRID_W)
    cs = np.clip(qc - NA_COLS // 2, 0, GRID_W - NA_COLS)
    valid = (kc[None, :] >= cs[:, None]) & (kc[None, :] < cs[:, None] + NA_COLS)
    dc = kc[None, :] - qc[:, None] + (NA_COLS - 1)
    d = np.arange(NA_ROWS)
    a = np.arange(NA_ROWS)
    dr = a[None, :] - d[:, None] + (NA_ROWS - 1)
    pick_r = (dr[:, :, None] == np.arange(2 * NA_ROWS - 1)).astype(np.float32)
    pick_c = ((dc[:, :, None] == np.arange(2 * NA_COLS - 1)) & valid[:, :, None]).astype(np.float32)
    tbl = jnp.einsum("hrc,dar,qkc->dhqak", rpb.astype(F32) * LOG2E, pick_r, pick_c,
                     precision=lax.Precision.HIGHEST)
    tbl = jnp.where(valid[None, None, :, None, :], tbl, NEG_BIG)
    return tbl.reshape(NA_ROWS, C_HEADS, GRID_W, NA_ROWS * GRID_W)


def _channel_kernel(h_ref, ya_ref, yb_ref, yc_ref, yd_ref, wout_ref, gffn_ref, wg_ref, wu_ref,
                    wd_ref, gple_ref, wpg_ref, p_ref, wpp_ref, gfin_ref, o_ref, *, final):
    w = MIXER_WIDTH
    h = h_ref[...]
    for j, y_ref in enumerate((ya_ref, yb_ref, yc_ref, yd_ref)):
        h = h + jnp.dot(y_ref[...], wout_ref[j * w:(j + 1) * w, :], preferred_element_type=F32)

    hn = _rms(h, gffn_ref[...]).astype(BF16)
    ffn = wg_ref.shape[1]
    acc = jnp.zeros(h.shape, F32)
    for c in range(ffn // FFN_CHUNK):
        cols = slice(c * FFN_CHUNK, (c + 1) * FFN_CHUNK)
        gate = jnp.dot(hn, wg_ref[:, cols], preferred_element_type=F32)
        up = jnp.dot(hn, wu_ref[:, cols], preferred_element_type=F32)
        act = (gate * jax.nn.sigmoid(gate) * up).astype(BF16)
        acc = acc + jnp.dot(act, wd_ref[cols, :], preferred_element_type=F32)
    h = h + acc

    hn = _rms(h, gple_ref[...]).astype(BF16)
    gate = jax.nn.sigmoid(jnp.dot(hn, wpg_ref[...], preferred_element_type=F32))
    emb = jnp.dot(p_ref[...].astype(BF16), wpp_ref[...], preferred_element_type=F32)
    h = h + gate * emb
    if final:
        h = _rms(h, gfin_ref[...])
    o_ref[...] = h


def _channel_call(h, ya, yb, yc, yd, wout, gffn, wg, wu, wd, gple, wpg, p, wpp, gfin, final, seq):
    rows, d = h.shape
    tm = min(ROW_TILE, seq)
    row_spec = lambda w: pl.BlockSpec((tm, w), lambda i: (i, 0))
    resident = lambda a: pl.BlockSpec(a.shape, lambda i: (0,) * a.ndim,
                                      pipeline_mode=pl.Buffered(1))
    return pl.pallas_call(
        functools.partial(_channel_kernel, final=final),
        grid=(rows // tm,),
        in_specs=[row_spec(d)] + [row_spec(MIXER_WIDTH)] * 4
                 + [resident(wout), resident(gffn), resident(wg), resident(wu), resident(wd),
                    resident(gple), resident(wpg), row_spec(p.shape[1]), resident(wpp),
                    resident(gfin)],
        out_specs=row_spec(d),
        out_shape=jax.ShapeDtypeStruct((rows, d), F32),
        compiler_params=pltpu.CompilerParams(
            dimension_semantics=("parallel",), vmem_limit_bytes=VMEM_LIMIT_BYTES),
        name="channel_mixers",
    )(h, ya, yb, yc, yd, wout, gffn, wg, wu, wd, gple, wpg, p, wpp, gfin)


def _rope_tables(seq):
    t = jnp.arange(seq)
    row = (t // GRID_W).astype(F32)
    col = (t % GRID_W).astype(F32)
    n_freq = HEAD_DIM // 4
    inv = ROPE_THETA ** (-jnp.arange(n_freq, dtype=F32) / n_freq)
    ang = jnp.concatenate([row[:, None] * inv, col[:, None] * inv], axis=-1)
    cos, sin = jnp.cos(ang), jnp.sin(ang)
    heads = LANES // HEAD_DIM
    return (jnp.tile(jnp.concatenate([cos, cos], axis=-1), (1, heads)),
            jnp.tile(jnp.concatenate([-sin, sin], axis=-1), (1, heads)))


def kernel(x, p, g_mix, w_in, a_q_norm, a_k_norm, b_lam_q, b_lam_k, b_sub_norm, c_rpb, d_ln_g,
           d_ln_b, d_w_s, d_b_s, w_out, g_ffn, w_gate, w_up, w_down, g_ple, w_ple_gate,
           w_ple_proj, g_final):
    b, seq, d = x.shape
    depth = w_in.shape[0]
    assert seq % GRID_W == 0 and seq // GRID_W >= NA_ROWS and seq % D_CHUNK == 0
    rows = b * seq
    h = x.reshape(rows, d)
    cos_t, sin_t = _rope_tables(seq)
    slopes = (2.0 ** (-8.0 / B_HEADS)) ** jnp.arange(1, B_HEADS + 1, dtype=F32) * LOG2E
    row2 = lambda a: a.reshape(1, -1).astype(F32)
    for i in range(depth):
        bs_tile = jnp.repeat(d_b_s[i].T, HEAD_DIM, axis=1)
        aqt, ak, avt, bqt, bk, bvt, cq, ck, cv, yd = _proj_call(
            h, row2(g_mix[i]), w_in[i].astype(BF16),
            row2(jnp.tile(a_q_norm[i], A_HEADS)), row2(jnp.tile(a_k_norm[i], A_KV_HEADS)),
            cos_t, sin_t, row2(d_ln_g[i]), row2(d_ln_b[i]), d_w_s[i].astype(BF16),
            bs_tile.astype(F32), seq)
        sh = lambda a: a.reshape(b, seq, MIXER_WIDTH)
        vt4 = lambda a: a.reshape(b, -1, a.shape[-2], a.shape[-1])
        ya = _gqa_call(aqt, sh(ak), vt4(avt))
        lam_init = 0.8 - 0.6 * math.exp(-0.3 * i)
        scal = jnp.concatenate([jnp.full((1,), lam_init, F32), slopes])
        yb = _diff_call(scal, bqt, sh(bk), vt4(bvt), b_lam_q[i].astype(F32),
                        b_lam_k[i].astype(F32), row2(jnp.tile(b_sub_norm[i], LANES // HEAD_DIM)))
        yc = _nbr_call(sh(cq), sh(ck), sh(cv), _nbr_bias_table(c_rpb[i]))
        fl = lambda a: a.reshape(rows, MIXER_WIDTH)
        h = _channel_call(
            h, fl(ya), fl(yb), fl(yc), yd, w_out[i].astype(BF16), row2(g_ffn[i]),
            w_gate[i].astype(BF16), w_up[i].astype(BF16), w_down[i].astype(BF16),
            row2(g_ple[i]), w_ple_gate[i].astype(BF16), p[i].reshape(rows, -1),
            w_ple_proj[i].astype(BF16), row2(g_final), i == depth - 1, seq)
    return h.reshape(b, seq, d)
```

```python
import functools
import math

import jax
import jax.numpy as jnp
import numpy as np
from jax import lax
from jax.experimental import pallas as pl
from jax.experimental.pallas import tpu as pltpu

F32 = jnp.float32
BF16 = jnp.bfloat16

HEAD_DIM = 64
GRID_W = 64
RMS_EPS = 1e-6
LN_EPS = 1e-5
A_HEADS = 4
A_KV_HEADS = 2
ROPE_THETA = 10000.0
B_HEADS = 4
B_QK_DIM = HEAD_DIM // 2
C_HEADS = 4
NA_ROWS = 8
NA_COLS = 16
D_GROUPS = 4
D_CHUNK = 128
MIXER_WIDTH = 256

LOG2E = math.log2(math.e)
NEG_BIG = -1e30

LANES = 128
BF16_SUBLANES = 16
VMEM_LIMIT_BYTES = 56 * 1024 * 1024

VT_ROWS = HEAD_DIM + BF16_SUBLANES

ROW_TILE = 512
A_Q_TILE = 256
B_Q_TILE = 128
C_BLOCK_ROWS = 8
FFN_CHUNK = 256
PIPELINE_UNROLL = 16
SCORE_SLOTS = 4
A_LOOKAHEAD = 2
B_LOOKAHEAD = 1


def _group_ones(width, group):
    r = lax.broadcasted_iota(jnp.int32, (width, width), 0) // group
    c = lax.broadcasted_iota(jnp.int32, (width, width), 1) // group
    return (r == c).astype(F32)


def _group_sum(x, group):
    return jnp.dot(x, _group_ones(x.shape[-1], group), precision=lax.Precision.HIGHEST,
                   preferred_element_type=F32)


def _rms(x, g):
    ms = jnp.mean(x * x, axis=-1, keepdims=True)
    return x * lax.rsqrt(ms + RMS_EPS) * g


def _proj_kernel(h_ref, gmix_ref, win_ref, gq_ref, gk_ref, cos_ref, sin_ref,
                 lng_ref, lnb_ref, ws_ref, bs_ref,
                 aqt_o, ak_o, avt_o, bqt_o, bk_o, bvt_o, cq_o, ck_o, cv_o, yd_o):
    tm = h_ref.shape[0]
    hn = _rms(h_ref[...], gmix_ref[...]).astype(BF16)

    def proj(lo, width):
        return jnp.dot(hn, win_ref[:, lo:lo + width], preferred_element_type=F32)

    cos = cos_ref[...]
    sin = sin_ref[...]

    def qk_norm_rope(y, gain, cos_t, sin_t):
        w = y.shape[-1]
        ss = _group_sum(y * y, HEAD_DIM)
        y = y * lax.rsqrt(ss * (1.0 / HEAD_DIM) + RMS_EPS) * gain
        lane = lax.broadcasted_iota(jnp.int32, y.shape, 1)
        half = HEAD_DIM // 2
        swapped = jnp.where(lane % HEAD_DIM < half,
                            pltpu.roll(y, w - half, axis=1), pltpu.roll(y, half, axis=1))
        return y * cos_t + swapped * sin_t

    a_scale = HEAD_DIM ** -0.5 * LOG2E
    cos2 = jnp.concatenate([cos, cos], axis=1)
    sin2 = jnp.concatenate([sin, sin], axis=1)
    aq = qk_norm_rope(proj(0, 256), gq_ref[...], cos2, sin2) * a_scale
    def store_tiles(o_ref, xt):
        tq = o_ref.shape[-1]
        for j in range(o_ref.shape[0]):
            o_ref[j] = xt[:, j * tq:(j + 1) * tq].astype(BF16)

    store_tiles(aqt_o, aq.T)
    ak = qk_norm_rope(proj(256, 128), gk_ref[...], cos, sin)
    av = proj(384, 128)
    lane128 = lax.broadcasted_iota(jnp.int32, (tm, LANES), 1)
    first = lane128 < HEAD_DIM

    def dup_heads(y):
        r = pltpu.roll(y, HEAD_DIM, axis=1)
        return jnp.concatenate([jnp.where(first, y, r), jnp.where(first, r, y)], axis=1)

    ak_o[...] = dup_heads(ak).astype(BF16)
    ones = jnp.ones((BF16_SUBLANES, tm), F32)
    avt = av.T
    avt_o[...] = jnp.concatenate(
        [avt[:HEAD_DIM], ones, avt[HEAD_DIM:], ones], axis=0).astype(BF16)

    store_tiles(bqt_o, (proj(512, 256) * (B_QK_DIM ** -0.5 * LOG2E)).T)
    bk_o[...] = proj(768, 256).astype(BF16)
    bvt = proj(1024, 256).T
    bvt_o[...] = jnp.concatenate(
        [x for hd in range(B_HEADS) for x in (bvt[hd * HEAD_DIM:(hd + 1) * HEAD_DIM], ones)],
        axis=0).astype(BF16)

    cq_o[...] = (proj(1280, 256) * a_scale).astype(BF16)
    ck_o[...] = proj(1536, 256).astype(BF16)
    cv_o[...] = proj(1792, 256).astype(BF16)

    def gelu(t):
        return 0.5 * t * (1.0 + lax.erf(t * (2.0 ** -0.5)))

    u = gelu(proj(2048, 256))
    vv = gelu(proj(2304, 256))
    mu = jnp.mean(vv, axis=-1, keepdims=True)
    vc = vv - mu
    var = jnp.mean(vc * vc, axis=-1, keepdims=True)
    vv = (vc * lax.rsqrt(var + LN_EPS) * lng_ref[...] + lnb_ref[...]).astype(BF16)
    lane_group = lax.broadcasted_iota(jnp.int32, (D_CHUNK, MIXER_WIDTH), 1) // HEAD_DIM
    for n in range(tm // D_CHUNK):
        rows = slice(n * D_CHUNK, (n + 1) * D_CHUNK)
        vchunk = vv[rows, :]
        sv = bs_ref[...]
        for g in range(D_GROUPS):
            mixed = jnp.dot(ws_ref[g], vchunk, preferred_element_type=F32)
            sv = sv + jnp.where(lane_group == g, mixed, 0.0)
        yd_o[rows, :] = (u[rows, :] * sv).astype(BF16)


def _proj_call(h, gmix, win, gq, gk, cos_t, sin_t, lng, lnb, ws, bs, seq):
    rows, d = h.shape
    tm = min(ROW_TILE, seq)
    steps_per_seq = seq // tm
    row_spec = lambda w: pl.BlockSpec((tm, w), lambda i: (i, 0))
    full = lambda a: pl.BlockSpec(a.shape, lambda i: (0,) * a.ndim)
    tab_spec = pl.BlockSpec((tm, LANES), lambda i: (i % steps_per_seq, 0))
    row_out = (row_spec(MIXER_WIDTH), jax.ShapeDtypeStruct((rows, MIXER_WIDTH), BF16))
    def qt_out(tile):
        tq = min(tile, seq)
        return (pl.BlockSpec((tm // tq, MIXER_WIDTH, tq), lambda i: (i, 0, 0)),
                jax.ShapeDtypeStruct((rows // tq, MIXER_WIDTH, tq), BF16))
    def vt_out(heads):
        r = heads * VT_ROWS
        return (pl.BlockSpec((None, r, tm), lambda i: (i, 0, 0)),
                jax.ShapeDtypeStruct((rows // tm, r, tm), BF16))
    outs = ([qt_out(A_Q_TILE), row_out, vt_out(A_KV_HEADS),
             qt_out(B_Q_TILE), row_out, vt_out(B_HEADS)] + [row_out] * 4)
    return pl.pallas_call(
        _proj_kernel,
        grid=(rows // tm,),
        in_specs=[row_spec(d), full(gmix), full(win), full(gq), full(gk), tab_spec, tab_spec,
                  full(lng), full(lnb), full(ws), full(bs)],
        out_specs=[o[0] for o in outs],
        out_shape=[o[1] for o in outs],
        compiler_params=pltpu.CompilerParams(
            dimension_semantics=("parallel",), vmem_limit_bytes=VMEM_LIMIT_BYTES),
        name="proj_mixers",
    )(h, gmix, win, gq, gk, cos_t, sin_t, lng, lnb, ws, bs)


def _online_softmax_step(s, m_cur, vts, m_scr, acc_scr, col_offset=None):
    m_prev = m_scr[...]
    if col_offset is not None:
        m_cur = m_cur - col_offset
    m_next = jnp.maximum(m_prev, m_cur)
    alpha = jnp.exp2(m_prev - m_next)
    shift = m_next if col_offset is None else m_next + col_offset
    p = jnp.exp2(s - shift).astype(BF16)
    width = p.shape[1] // len(vts)
    pv = [jnp.dot(vt, p[:, j * width:(j + 1) * width], preferred_element_type=F32)
          for j, vt in enumerate(vts)]
    acc_scr[...] = alpha * acc_scr[...] + jnp.concatenate(pv, axis=1)
    m_scr[...] = m_next


def _stack_masked_t(qt, groups):
    qt = qt.astype(F32)
    width = qt.shape[0] // groups
    row = lax.broadcasted_iota(jnp.int32, qt.shape, 0) // width
    zero = jnp.zeros_like(qt)
    return jnp.concatenate([jnp.where(row == j, qt, zero) for j in range(groups)],
                           axis=1).astype(BF16)


def _stack_masked(q, groups):
    width = q.shape[1] // groups
    lane = lax.broadcasted_iota(jnp.int32, q.shape, 1) // width
    zero = jnp.zeros_like(q)
    return jnp.concatenate([jnp.where(lane == j, q, zero) for j in range(groups)], axis=0)


def _init_softmax_state(m_scr, acc_scr):
    m_scr[...] = jnp.full(m_scr.shape, NEG_BIG, F32)
    acc_scr[...] = jnp.zeros(acc_scr.shape, F32)


def _normalized(acc_scr):
    return acc_scr[:HEAD_DIM, :] / acc_scr[HEAD_DIM:HEAD_DIM + 1, :]


def _dense_specs(b, seq, tq, nk, vt_rows, tk):
    qt_spec = pl.BlockSpec((None, seq // tq, LANES, tq), lambda bi, g, qi: (bi, 0, g, 0))
    k_spec = pl.BlockSpec((None, seq, LANES), lambda bi, g, qi: (bi, 0, g))
    vt_spec = pl.BlockSpec((None, nk, vt_rows, tk), lambda bi, g, qi: (bi, 0, g, 0))
    o_spec = pl.BlockSpec((None, tq, LANES), lambda bi, g, qi: (bi, qi, g))
    return qt_spec, k_spec, vt_spec, o_spec


def _pipelined_chunks(nk, lookahead, first_tile, scores, consume, s_scr, max_scr):
    unroll = PIPELINE_UNROLL if nk % PIPELINE_UNROLL == 0 else s_scr.shape[0]
    slots = s_scr.shape[0]
    assert nk % unroll == 0 and unroll % slots == 0 and lookahead < slots <= nk

    def produce(ahead, c, slot):
        s = scores(ahead, c)
        s_scr[slot] = s
        max_scr[slot] = jnp.max(s, axis=0, keepdims=True)

    @pl.when(first_tile)
    def _():
        for c in range(lookahead):
            produce(0, c, c)

    def body(i, carry):
        for j in range(unroll):
            c = unroll * i + j
            ahead = c + lookahead
            if j + lookahead < unroll:
                produce(0, ahead, (j + lookahead) % slots)
            else:
                wrap = (ahead >= nk).astype(jnp.int32)
                produce(wrap, ahead - wrap * nk, (j + lookahead) % slots)
            consume(c, s_scr[j % slots], max_scr[j % slots])
        return carry

    lax.fori_loop(0, nk // unroll, body, 0)


def _stage_query_tiles(qt_ref, qs_scr, qi, groups):
    last = qt_ref.shape[0] - 1

    @pl.when(qi == 0)
    def _():
        qs_scr[0] = _stack_masked_t(qt_ref[0], groups)

    nxt = jnp.minimum(qi + 1, last)
    qs_scr[(qi + 1) % 2] = _stack_masked_t(qt_ref[nxt], groups)
    return nxt


def _gqa_kernel(qt_ref, k_ref, vt_ref, o_ref, qs_scr, s_scr, max_scr, m_scr, acc_scr):
    tq = qt_ref.shape[2]
    nk, _, tk = vt_ref.shape
    qi = pl.program_id(2)
    _stage_query_tiles(qt_ref, qs_scr, qi, 2)
    _init_softmax_state(m_scr, acc_scr)

    def scores(ahead, c):
        start = pl.multiple_of(c * tk, tk)
        return jnp.dot(k_ref[pl.ds(start, tk), :], qs_scr[(qi + ahead) % 2],
                       preferred_element_type=F32)

    def consume(c, s, m_cur):
        _online_softmax_step(s, m_cur, [vt_ref[c]], m_scr, acc_scr)

    _pipelined_chunks(nk, A_LOOKAHEAD, qi == 0, scores, consume, s_scr, max_scr)
    o = _normalized(acc_scr)
    o_ref[...] = jnp.concatenate([o[:, :tq], o[:, tq:]], axis=0).T.astype(o_ref.dtype)


def _gqa_call(qt, k, vt):
    b, nk, vt_rows, tk = vt.shape
    vt_rows //= A_KV_HEADS
    seq = k.shape[1]
    tq = min(A_Q_TILE, seq)
    qt_spec, k_spec, vt_spec, o_spec = _dense_specs(b, seq, tq, nk, vt_rows, tk)
    return pl.pallas_call(
        _gqa_kernel,
        grid=(b, A_KV_HEADS, seq // tq),
        in_specs=[qt_spec, k_spec, vt_spec],
        out_specs=o_spec,
        out_shape=jax.ShapeDtypeStruct((b, seq, MIXER_WIDTH), BF16),
        scratch_shapes=[pltpu.VMEM((2, LANES, 2 * tq), BF16),
                        pltpu.VMEM((SCORE_SLOTS, tk, 2 * tq), F32),
                        pltpu.VMEM((SCORE_SLOTS, 1, 2 * tq), F32),
                        pltpu.VMEM((1, 2 * tq), F32),
                        pltpu.VMEM((VT_ROWS, 2 * tq), F32)],
        compiler_params=pltpu.CompilerParams(
            dimension_semantics=("parallel", "parallel", "arbitrary"),
            vmem_limit_bytes=VMEM_LIMIT_BYTES),
        name="gqa_attention",
    )(qt, k, vt)


def _diff_kernel(scal_ref, qt_ref, k_ref, vt_ref, lq_ref, lk_ref, gsub_ref, o_ref,
                 qs_scr, s_scr, max_scr, key_scr, m_scr, acc_scr):
    tq = qt_ref.shape[2]
    nk, _, tk = vt_ref.shape
    pair = pl.program_id(1)
    qi = pl.program_id(2)
    lam_init = scal_ref[0]
    slope_a = scal_ref[1 + 2 * pair]
    slope_b = scal_ref[2 + 2 * pair]
    nxt = _stage_query_tiles(qt_ref, qs_scr, qi, 4)
    _init_softmax_state(m_scr, acc_scr)

    q0 = qi * tq
    c_mid = q0 // tk
    col = lax.broadcasted_iota(jnp.int32, (1, 4 * tq), 1)
    col_slope = jnp.where(col < 2 * tq, slope_a, slope_b)
    col_pos = (q0 + col % tq).astype(F32)
    key_off = lax.broadcasted_iota(jnp.int32, (tk, 4 * tq), 0)

    tiles_per_chunk = tk // tq

    @pl.when(qi == 0)
    def _():
        key_table = key_off.astype(F32) * col_slope
        key_scr[0] = key_table
        key_scr[1] = -key_table
        rel = lax.broadcasted_iota(jnp.int32, (tk, 4 * tq), 1) % tq - key_off
        for t in range(tiles_per_chunk):
            key_scr[2 + t] = -(jnp.abs(rel + t * tq).astype(F32) * col_slope)

    def scores(ahead, c):
        tile = jnp.where(ahead == 0, qi, nxt)
        mid = tile // tiles_per_chunk
        table = jnp.where(c == mid, 2 + tile % tiles_per_chunk, jnp.where(c < mid, 0, 1))
        start = pl.multiple_of(c * tk, tk)
        s = jnp.dot(k_ref[pl.ds(start, tk), :], qs_scr[(qi + ahead) % 2],
                    preferred_element_type=F32)
        return s + key_scr[table]

    def consume(c, s, m_cur):
        sign = jnp.where(c == c_mid, 0.0, jnp.where(c < c_mid, 1.0, -1.0)).astype(F32)
        col_offset = col_slope * (sign * (col_pos - (c * tk).astype(F32)))
        vts = [vt_ref[c, :VT_ROWS, :], vt_ref[c, VT_ROWS:, :]]
        _online_softmax_step(s, m_cur, vts, m_scr, acc_scr, col_offset)

    _pipelined_chunks(nk, B_LOOKAHEAD, qi == 0, scores, consume, s_scr, max_scr)

    t = jnp.exp(jnp.sum(lq_ref[...] * lk_ref[...], axis=1, keepdims=True))
    lam = t[0:1, :] - t[1:2, :] + lam_init
    o = _normalized(acc_scr)
    o_a = o[:, 0:tq] - lam * o[:, tq:2 * tq]
    o_b = o[:, 2 * tq:3 * tq] - lam * o[:, 3 * tq:]
    o = jnp.concatenate([o_a, o_b], axis=0).T
    ss = _group_sum(o * o, HEAD_DIM)
    o = o * lax.rsqrt(ss * (1.0 / HEAD_DIM) + RMS_EPS) * gsub_ref[...] * (1.0 - lam_init)
    o_ref[...] = o.astype(o_ref.dtype)


def _diff_call(scal, qt, k, vt, lq, lk, gsub):
    b, nk, vt_rows, tk = vt.shape
    vt_rows //= B_HEADS // 2
    seq = k.shape[1]
    tq = min(B_Q_TILE, seq)
    assert tk % tq == 0
    qt_spec, k_spec, vt_spec, o_spec = _dense_specs(b, seq, tq, nk, vt_rows, tk)
    full = lambda a: pl.BlockSpec(a.shape, lambda bi, g, qi: (0,) * a.ndim)
    return pl.pallas_call(
        _diff_kernel,
        grid=(b, B_HEADS // 2, seq // tq),
        in_specs=[pl.BlockSpec(memory_space=pltpu.SMEM), qt_spec, k_spec, vt_spec,
                  full(lq), full(lk), full(gsub)],
        out_specs=o_spec,
        out_shape=jax.ShapeDtypeStruct((b, seq, MIXER_WIDTH), BF16),
        scratch_shapes=[pltpu.VMEM((2, LANES, 4 * tq), BF16),
                        pltpu.VMEM((SCORE_SLOTS, tk, 4 * tq), F32),
                        pltpu.VMEM((SCORE_SLOTS, 1, 4 * tq), F32),
                        pltpu.VMEM((2 + tk // tq, tk, 4 * tq), F32),
                        pltpu.VMEM((1, 4 * tq), F32),
                        pltpu.VMEM((VT_ROWS, 4 * tq), F32)],
        compiler_params=pltpu.CompilerParams(
            dimension_semantics=("parallel", "parallel", "arbitrary"),
            vmem_limit_bytes=VMEM_LIMIT_BYTES),
        name="diff_attention",
    )(scal, qt, k, vt, lq, lk, gsub)


def _nbr_kernel(q_ref, kp_ref, kc_ref, kn_ref, vp_ref, vc_ref, vn_ref, bias_ref, o_ref,
                kband, vband, *, grid_rows):
    blk = q_ref.shape[0]
    rb = blk // GRID_W
    i = pl.program_id(1)
    kband[0:blk] = kp_ref[...]
    kband[blk:2 * blk] = kc_ref[...]
    kband[2 * blk:] = kn_ref[...]
    vband[0:blk] = vp_ref[...]
    vband[blk:2 * blk] = vc_ref[...]
    vband[2 * blk:] = vn_ref[...]
    band = NA_ROWS * GRID_W
    lane_head = lax.broadcasted_iota(jnp.int32, (GRID_W, MIXER_WIDTH), 1) // HEAD_DIM
    for jr in range(rb):
        r = i * rb + jr
        rs = jnp.clip(r - NA_ROWS // 2, 0, grid_rows - NA_ROWS)
        start = pl.multiple_of((rs - (i - 1) * rb) * GRID_W, GRID_W)
        k = kband[pl.ds(start, band), :]
        v = vband[pl.ds(start, band), :]
        qs = _stack_masked(q_ref[jr * GRID_W:(jr + 1) * GRID_W, :], C_HEADS)
        s = lax.dot_general(qs, k, (((1,), (1,)), ((), ())), preferred_element_type=F32)
        s = s + bias_ref[r - rs].reshape(C_HEADS * GRID_W, band)
        m = jnp.max(s, axis=1, keepdims=True)
        p = jnp.exp2(s - m)
        l = jnp.sum(p, axis=1, keepdims=True)
        pv = jnp.dot(p.astype(BF16), v, preferred_element_type=F32) / l
        o = jnp.zeros((GRID_W, MIXER_WIDTH), F32)
        for hd in range(C_HEADS):
            o = jnp.where(lane_head == hd, pv[hd * GRID_W:(hd + 1) * GRID_W], o)
        o_ref[jr * GRID_W:(jr + 1) * GRID_W, :] = o.astype(o_ref.dtype)


def _nbr_call(q, k, v, bias):
    b, seq, w = q.shape
    grid_rows = seq // GRID_W
    rb = min(C_BLOCK_ROWS, grid_rows)
    blk = rb * GRID_W
    nblk = seq // blk
    cur = pl.BlockSpec((None, blk, w), lambda bi, i: (bi, i, 0))
    prev = pl.BlockSpec((None, blk, w), lambda bi, i: (bi, jnp.maximum(i - 1, 0), 0))
    nxt = pl.BlockSpec((None, blk, w), lambda bi, i: (bi, jnp.minimum(i + 1, nblk - 1), 0))
    return pl.pallas_call(
        functools.partial(_nbr_kernel, grid_rows=grid_rows),
        grid=(b, nblk),
        in_specs=[cur, prev, cur, nxt, prev, cur, nxt,
                  pl.BlockSpec(bias.shape, lambda bi, i: (0, 0, 0, 0))],
        out_specs=cur,
        out_shape=jax.ShapeDtypeStruct((b, seq, w), BF16),
        scratch_shapes=[pltpu.VMEM((3 * blk, w), BF16), pltpu.VMEM((3 * blk, w), BF16)],
        compiler_params=pltpu.CompilerParams(
            dimension_semantics=("parallel", "parallel"), vmem_limit_bytes=VMEM_LIMIT_BYTES),
        name="nbr_attention",
    )(q, k, k, k, v, v, v, bias)


def _nbr_bias_table(rpb):
    qc = np.arange(GRID_W)
    kc = np.arange(GRID_W)
    cs = np.clip(qc - NA_COLS // 2, 0, GRID_W - NA_COLS)
    valid = (kc[None, :] >= cs[:, None]) & (kc[None, :] < cs[:, None] + NA_COLS)
    dc = kc[None, :] - qc[:, None] + (NA_COLS - 1)
    d = np.arange(NA_ROWS)
    a = np.arange(NA_ROWS)
    dr = a[None, :] - d[:, None] + (NA_ROWS - 1)
    pick_r = (dr[:, :, None] == np.arange(2 * NA_ROWS - 1)).astype(np.float32)
    pick_c = ((dc[:, :, None] == np.arange(2 * NA_COLS - 1)) & valid[:, :, None]).astype(np.float32)
    tbl = jnp.einsum("hrc,dar,qkc->dhqak", rpb.astype(F32) * LOG2E, pick_r, pick_c,
                     precision=lax.Precision.HIGHEST)
    tbl = jnp.where(valid[None, None, :, None, :], tbl, NEG_BIG)
    return tbl.reshape(NA_ROWS, C_HEADS, GRID_W, NA_ROWS * GRID_W)


def _channel_kernel(h_ref, ya_ref, yb_ref, yc_ref, yd_ref, wout_ref, gffn_ref, wg_ref, wu_ref,
                    wd_ref, gple_ref, wpg_ref, p_ref, wpp_ref, gfin_ref, o_ref, *, final):
    w = MIXER_WIDTH
    h = h_ref[...]
    for j, y_ref in enumerate((ya_ref, yb_ref, yc_ref, yd_ref)):
        h = h + jnp.dot(y_ref[...], wout_ref[j * w:(j + 1) * w, :], preferred_element_type=F32)

    hn = _rms(h, gffn_ref[...]).astype(BF16)
    ffn = wg_ref.shape[1]
    acc = jnp.zeros(h.shape, F32)
    for c in range(ffn // FFN_CHUNK):
        cols = slice(c * FFN_CHUNK, (c + 1) * FFN_CHUNK)
        gate = jnp.dot(hn, wg_ref[:, cols], preferred_element_type=F32)
        up = jnp.dot(hn, wu_ref[:, cols], preferred_element_type=F32)
        act = (gate * jax.nn.sigmoid(gate) * up).astype(BF16)
        acc = acc + jnp.dot(act, wd_ref[cols, :], preferred_element_type=F32)
    h = h + acc

    hn = _rms(h, gple_ref[...]).astype(BF16)
    gate = jax.nn.sigmoid(jnp.dot(hn, wpg_ref[...], preferred_element_type=F32))
    emb = jnp.dot(p_ref[...].astype(BF16), wpp_ref[...], preferred_element_type=F32)
    h = h + gate * emb
    if final:
        h = _rms(h, gfin_ref[...])
    o_ref[...] = h


def _channel_call(h, ya, yb, yc, yd, wout, gffn, wg, wu, wd, gple, wpg, p, wpp, gfin, final, seq):
    rows, d = h.shape
    tm = min(ROW_TILE, seq)
    row_spec = lambda w: pl.BlockSpec((tm, w), lambda i: (i, 0))
    resident = lambda a: pl.BlockSpec(a.shape, lambda i: (0,) * a.ndim,
                                      pipeline_mode=pl.Buffered(1))
    return pl.pallas_call(
        functools.partial(_channel_kernel, final=final),
        grid=(rows // tm,),
        in_specs=[row_spec(d)] + [row_spec(MIXER_WIDTH)] * 4
                 + [resident(wout), resident(gffn), resident(wg), resident(wu), resident(wd),
                    resident(gple), resident(wpg), row_spec(p.shape[1]), resident(wpp),
                    resident(gfin)],
        out_specs=row_spec(d),
        out_shape=jax.ShapeDtypeStruct((rows, d), F32),
        compiler_params=pltpu.CompilerParams(
            dimension_semantics=("parallel",), vmem_limit_bytes=VMEM_LIMIT_BYTES),
        name="channel_mixers",
    )(h, ya, yb, yc, yd, wout, gffn, wg, wu, wd, gple, wpg, p, wpp, gfin)


def _rope_tables(seq):
    t = jnp.arange(seq)
    row = (t // GRID_W).astype(F32)
    col = (t % GRID_W).astype(F32)
    n_freq = HEAD_DIM // 4
    inv = ROPE_THETA ** (-jnp.arange(n_freq, dtype=F32) / n_freq)
    ang = jnp.concatenate([row[:, None] * inv, col[:, None] * inv], axis=-1)
    cos, sin = jnp.cos(ang), jnp.sin(ang)
    heads = LANES // HEAD_DIM
    return (jnp.tile(jnp.concatenate([cos, cos], axis=-1), (1, heads)),
            jnp.tile(jnp.concatenate([-sin, sin], axis=-1), (1, heads)))


def kernel(x, p, g_mix, w_in, a_q_norm, a_k_norm, b_lam_q, b_lam_k, b_sub_norm, c_rpb, d_ln_g,
           d_ln_b, d_w_s, d_b_s, w_out, g_ffn, w_gate, w_up, w_down, g_ple, w_ple_gate,
           w_ple_proj, g_final):
    b, seq, d = x.shape
    depth = w_in.shape[0]
    assert seq % GRID_W == 0 and seq // GRID_W >= NA_ROWS and seq % D_CHUNK == 0
    rows = b * seq
    h = x.reshape(rows, d)
    cos_t, sin_t = _rope_tables(seq)
    slopes = (2.0 ** (-8.0 / B_HEADS)) ** jnp.arange(1, B_HEADS + 1, dtype=F32) * LOG2E
    row2 = lambda a: a.reshape(1, -1).astype(F32)
    for i in range(depth):
        bs_tile = jnp.repeat(d_b_s[i].T, HEAD_DIM, axis=1)
        aqt, ak, avt, bqt, bk, bvt, cq, ck, cv, yd = _proj_call(
            h, row2(g_mix[i]), w_in[i].astype(BF16),
            row2(jnp.tile(a_q_norm[i], A_HEADS)), row2(jnp.tile(a_k_norm[i], A_KV_HEADS)),
            cos_t, sin_t, row2(d_ln_g[i]), row2(d_ln_b[i]), d_w_s[i].astype(BF16),
            bs_tile.astype(F32), seq)
        sh = lambda a: a.reshape(b, seq, MIXER_WIDTH)
        vt4 = lambda a: a.reshape(b, -1, a.shape[-2], a.shape[-1])
        ya = _gqa_call(vt4(aqt), sh(ak), vt4(avt))
        lam_init = 0.8 - 0.6 * math.exp(-0.3 * i)
        scal = jnp.concatenate([jnp.full((1,), lam_init, F32), slopes])
        yb = _diff_call(scal, vt4(bqt), sh(bk), vt4(bvt), b_lam_q[i].astype(F32),
                        b_lam_k[i].astype(F32), row2(jnp.tile(b_sub_norm[i], LANES // HEAD_DIM)))
        yc = _nbr_call(sh(cq), sh(ck), sh(cv), _nbr_bias_table(c_rpb[i]))
        fl = lambda a: a.reshape(rows, MIXER_WIDTH)
        h = _channel_call(
            h, fl(ya), fl(yb), fl(yc), yd, w_out[i].astype(BF16), row2(g_ffn[i]),
            w_gate[i].astype(BF16), w_up[i].astype(BF16), w_down[i].astype(BF16),
            row2(g_ple[i]), w_ple_gate[i].astype(BF16), p[i].reshape(rows, -1),
            w_ple_proj[i].astype(BF16), row2(g_final), i == depth - 1, seq)
    return h.reshape(b, seq, d)
```

```python
import functools
import math

import jax
import jax.numpy as jnp
import numpy as np
from jax import lax
from jax.experimental import pallas as pl
from jax.experimental.pallas import tpu as pltpu

F32 = jnp.float32
BF16 = jnp.bfloat16

HEAD_DIM = 64
GRID_W = 64
RMS_EPS = 1e-6
LN_EPS = 1e-5
A_HEADS = 4
A_KV_HEADS = 2
ROPE_THETA = 10000.0
B_HEADS = 4
B_QK_DIM = HEAD_DIM // 2
C_HEADS = 4
NA_ROWS = 8
NA_COLS = 16
D_GROUPS = 4
D_CHUNK = 128
MIXER_WIDTH = 256

LOG2E = math.log2(math.e)
NEG_BIG = -1e30

LANES = 128
BF16_SUBLANES = 16
VMEM_LIMIT_BYTES = 56 * 1024 * 1024

VT_ROWS = HEAD_DIM + BF16_SUBLANES

ROW_TILE = 512
A_Q_TILE = 256
B_Q_TILE = 128
C_BLOCK_ROWS = 8
FFN_CHUNK = 256
PIPELINE_UNROLL = 16
SCORE_SLOTS = 4
SLOT_SKEW = 8
A_LOOKAHEAD = 2
B_LOOKAHEAD = 1


def _group_ones(width, group):
    r = lax.broadcasted_iota(jnp.int32, (width, width), 0) // group
    c = lax.broadcasted_iota(jnp.int32, (width, width), 1) // group
    return (r == c).astype(F32)


def _group_sum(x, group):
    return jnp.dot(x, _group_ones(x.shape[-1], group), precision=lax.Precision.HIGHEST,
                   preferred_element_type=F32)


def _rms(x, g):
    ms = jnp.mean(x * x, axis=-1, keepdims=True)
    return x * lax.rsqrt(ms + RMS_EPS) * g


def _proj_kernel(h_ref, gmix_ref, win_ref, gq_ref, gk_ref, cos_ref, sin_ref,
                 lng_ref, lnb_ref, ws_ref, bs_ref,
                 aqt_o, ak_o, avt_o, bqt_o, bk_o, bvt_o, cq_o, ck_o, cv_o, yd_o):
    tm = h_ref.shape[0]
    hn = _rms(h_ref[...], gmix_ref[...]).astype(BF16)

    def proj(lo, width):
        return jnp.dot(hn, win_ref[:, lo:lo + width], preferred_element_type=F32)

    cos = cos_ref[...]
    sin = sin_ref[...]

    def qk_norm_rope(y, gain, cos_t, sin_t):
        w = y.shape[-1]
        ss = _group_sum(y * y, HEAD_DIM)
        y = y * lax.rsqrt(ss * (1.0 / HEAD_DIM) + RMS_EPS) * gain
        lane = lax.broadcasted_iota(jnp.int32, y.shape, 1)
        half = HEAD_DIM // 2
        swapped = jnp.where(lane % HEAD_DIM < half,
                            pltpu.roll(y, w - half, axis=1), pltpu.roll(y, half, axis=1))
        return y * cos_t + swapped * sin_t

    a_scale = HEAD_DIM ** -0.5 * LOG2E
    cos2 = jnp.concatenate([cos, cos], axis=1)
    sin2 = jnp.concatenate([sin, sin], axis=1)
    aq = qk_norm_rope(proj(0, 256), gq_ref[...], cos2, sin2) * a_scale
    def store_tiles(o_ref, xt):
        tq = o_ref.shape[-1]
        for j in range(o_ref.shape[0]):
            o_ref[j] = xt[:, j * tq:(j + 1) * tq].astype(BF16)

    store_tiles(aqt_o, aq.T)
    ak = qk_norm_rope(proj(256, 128), gk_ref[...], cos, sin)
    av = proj(384, 128)
    lane128 = lax.broadcasted_iota(jnp.int32, (tm, LANES), 1)
    first = lane128 < HEAD_DIM

    def dup_heads(y):
        r = pltpu.roll(y, HEAD_DIM, axis=1)
        return jnp.concatenate([jnp.where(first, y, r), jnp.where(first, r, y)], axis=1)

    ak_o[...] = dup_heads(ak).astype(BF16)
    ones = jnp.ones((BF16_SUBLANES, tm), F32)
    avt = av.T
    avt_o[...] = jnp.concatenate(
        [avt[:HEAD_DIM], ones, avt[HEAD_DIM:], ones], axis=0).astype(BF16)

    store_tiles(bqt_o, (proj(512, 256) * (B_QK_DIM ** -0.5 * LOG2E)).T)
    bk_o[...] = proj(768, 256).astype(BF16)
    bvt = proj(1024, 256).T
    bvt_o[...] = jnp.concatenate(
        [x for hd in range(B_HEADS) for x in (bvt[hd * HEAD_DIM:(hd + 1) * HEAD_DIM], ones)],
        axis=0).astype(BF16)

    cq_o[...] = (proj(1280, 256) * a_scale).astype(BF16)
    ck_o[...] = proj(1536, 256).astype(BF16)
    cv_o[...] = proj(1792, 256).astype(BF16)

    def gelu(t):
        return 0.5 * t * (1.0 + lax.erf(t * (2.0 ** -0.5)))

    u = gelu(proj(2048, 256))
    vv = gelu(proj(2304, 256))
    mu = jnp.mean(vv, axis=-1, keepdims=True)
    vc = vv - mu
    var = jnp.mean(vc * vc, axis=-1, keepdims=True)
    vv = (vc * lax.rsqrt(var + LN_EPS) * lng_ref[...] + lnb_ref[...]).astype(BF16)
    lane_group = lax.broadcasted_iota(jnp.int32, (D_CHUNK, MIXER_WIDTH), 1) // HEAD_DIM
    for n in range(tm // D_CHUNK):
        rows = slice(n * D_CHUNK, (n + 1) * D_CHUNK)
        vchunk = vv[rows, :]
        sv = bs_ref[...]
        for g in range(D_GROUPS):
            mixed = jnp.dot(ws_ref[g], vchunk, preferred_element_type=F32)
            sv = sv + jnp.where(lane_group == g, mixed, 0.0)
        yd_o[rows, :] = (u[rows, :] * sv).astype(BF16)


def _proj_call(h, gmix, win, gq, gk, cos_t, sin_t, lng, lnb, ws, bs, seq):
    rows, d = h.shape
    tm = min(ROW_TILE, seq)
    steps_per_seq = seq // tm
    row_spec = lambda w: pl.BlockSpec((tm, w), lambda i: (i, 0))
    full = lambda a: pl.BlockSpec(a.shape, lambda i: (0,) * a.ndim)
    tab_spec = pl.BlockSpec((tm, LANES), lambda i: (i % steps_per_seq, 0))
    row_out = (row_spec(MIXER_WIDTH), jax.ShapeDtypeStruct((rows, MIXER_WIDTH), BF16))
    def qt_out(tile):
        tq = min(tile, seq)
        return (pl.BlockSpec((tm // tq, MIXER_WIDTH, tq), lambda i: (i, 0, 0)),
                jax.ShapeDtypeStruct((rows // tq, MIXER_WIDTH, tq), BF16))
    def vt_out(heads):
        r = heads * VT_ROWS
        return (pl.BlockSpec((None, r, tm), lambda i: (i, 0, 0)),
                jax.ShapeDtypeStruct((rows // tm, r, tm), BF16))
    outs = ([qt_out(A_Q_TILE), row_out, vt_out(A_KV_HEADS),
             qt_out(B_Q_TILE), row_out, vt_out(B_HEADS)] + [row_out] * 4)
    return pl.pallas_call(
        _proj_kernel,
        grid=(rows // tm,),
        in_specs=[row_spec(d), full(gmix), full(win), full(gq), full(gk), tab_spec, tab_spec,
                  full(lng), full(lnb), full(ws), full(bs)],
        out_specs=[o[0] for o in outs],
        out_shape=[o[1] for o in outs],
        compiler_params=pltpu.CompilerParams(
            dimension_semantics=("parallel",), vmem_limit_bytes=VMEM_LIMIT_BYTES),
        name="proj_mixers",
    )(h, gmix, win, gq, gk, cos_t, sin_t, lng, lnb, ws, bs)


def _online_softmax_step(s, m_cur, vts, m_scr, acc_scr, col_offset=None):
    m_prev = m_scr[...]
    if col_offset is not None:
        m_cur = m_cur - col_offset
    m_next = jnp.maximum(m_prev, m_cur)
    alpha = jnp.exp2(m_prev - m_next)
    shift = m_next if col_offset is None else m_next + col_offset
    p = jnp.exp2(s - shift).astype(BF16)
    width = p.shape[1] // len(vts)
    pv = [jnp.dot(vt, p[:, j * width:(j + 1) * width], preferred_element_type=F32)
          for j, vt in enumerate(vts)]
    acc_scr[...] = alpha * acc_scr[...] + jnp.concatenate(pv, axis=1)
    m_scr[...] = m_next


def _stack_masked_t(qt, groups):
    qt = qt.astype(F32)
    width = qt.shape[0] // groups
    row = lax.broadcasted_iota(jnp.int32, qt.shape, 0) // width
    zero = jnp.zeros_like(qt)
    return jnp.concatenate([jnp.where(row == j, qt, zero) for j in range(groups)],
                           axis=1).astype(BF16)


def _stack_masked(q, groups):
    width = q.shape[1] // groups
    lane = lax.broadcasted_iota(jnp.int32, q.shape, 1) // width
    zero = jnp.zeros_like(q)
    return jnp.concatenate([jnp.where(lane == j, q, zero) for j in range(groups)], axis=0)


def _init_softmax_state(m_scr, acc_scr):
    m_scr[...] = jnp.full(m_scr.shape, NEG_BIG, F32)
    acc_scr[...] = jnp.zeros(acc_scr.shape, F32)


def _normalized(acc_scr):
    return acc_scr[:HEAD_DIM, :] / acc_scr[HEAD_DIM:HEAD_DIM + 1, :]


def _dense_specs(b, seq, tq, nk, vt_rows, tk):
    qt_spec = pl.BlockSpec((None, seq // tq, LANES, tq), lambda bi, g, qi: (bi, 0, g, 0))
    k_spec = pl.BlockSpec((None, seq, LANES), lambda bi, g, qi: (bi, 0, g))
    vt_spec = pl.BlockSpec((None, nk, vt_rows, tk), lambda bi, g, qi: (bi, 0, g, 0))
    o_spec = pl.BlockSpec((None, tq, LANES), lambda bi, g, qi: (bi, qi, g))
    return qt_spec, k_spec, vt_spec, o_spec


def _pipelined_chunks(nk, lookahead, first_tile, scores, consume, s_scr, max_scr):
    unroll = PIPELINE_UNROLL if nk % PIPELINE_UNROLL == 0 else s_scr.shape[0]
    slots = s_scr.shape[0]
    assert nk % unroll == 0 and unroll % slots == 0 and lookahead < slots <= nk

    def produce(ahead, c, slot):
        s = scores(ahead, c)
        s_scr[slot, :s.shape[0], :] = s
        max_scr[slot] = jnp.max(s, axis=0, keepdims=True)

    @pl.when(first_tile)
    def _():
        for c in range(lookahead):
            produce(0, c, c)

    def body(i, carry):
        for j in range(unroll):
            c = unroll * i + j
            ahead = c + lookahead
            if j + lookahead < unroll:
                produce(0, ahead, (j + lookahead) % slots)
            else:
                wrap = (ahead >= nk).astype(jnp.int32)
                produce(wrap, ahead - wrap * nk, (j + lookahead) % slots)
            consume(c, s_scr[j % slots, :s_scr.shape[1] - SLOT_SKEW, :], max_scr[j % slots])
        return carry

    lax.fori_loop(0, nk // unroll, body, 0)


def _stage_query_tiles(qt_ref, qs_scr, qi, groups):
    last = qt_ref.shape[0] - 1

    @pl.when(qi == 0)
    def _():
        qs_scr[0] = _stack_masked_t(qt_ref[0], groups)

    nxt = jnp.minimum(qi + 1, last)
    qs_scr[(qi + 1) % 2] = _stack_masked_t(qt_ref[nxt], groups)
    return nxt


def _gqa_kernel(qt_ref, k_ref, vt_ref, o_ref, qs_scr, s_scr, max_scr, m_scr, acc_scr):
    tq = qt_ref.shape[2]
    nk, _, tk = vt_ref.shape
    qi = pl.program_id(2)
    _stage_query_tiles(qt_ref, qs_scr, qi, 2)
    _init_softmax_state(m_scr, acc_scr)

    def scores(ahead, c):
        start = pl.multiple_of(c * tk, tk)
        return jnp.dot(k_ref[pl.ds(start, tk), :], qs_scr[(qi + ahead) % 2],
                       preferred_element_type=F32)

    def consume(c, s, m_cur):
        _online_softmax_step(s, m_cur, [vt_ref[c]], m_scr, acc_scr)

    _pipelined_chunks(nk, A_LOOKAHEAD, qi == 0, scores, consume, s_scr, max_scr)
    o = _normalized(acc_scr)
    o_ref[...] = jnp.concatenate([o[:, :tq], o[:, tq:]], axis=0).T.astype(o_ref.dtype)


def _gqa_call(qt, k, vt):
    b, nk, vt_rows, tk = vt.shape
    vt_rows //= A_KV_HEADS
    seq = k.shape[1]
    tq = min(A_Q_TILE, seq)
    qt_spec, k_spec, vt_spec, o_spec = _dense_specs(b, seq, tq, nk, vt_rows, tk)
    return pl.pallas_call(
        _gqa_kernel,
        grid=(b, A_KV_HEADS, seq // tq),
        in_specs=[qt_spec, k_spec, vt_spec],
        out_specs=o_spec,
        out_shape=jax.ShapeDtypeStruct((b, seq, MIXER_WIDTH), BF16),
        scratch_shapes=[pltpu.VMEM((2, LANES, 2 * tq), BF16),
                        pltpu.VMEM((SCORE_SLOTS, tk + SLOT_SKEW, 2 * tq), F32),
                        pltpu.VMEM((SCORE_SLOTS, 1, 2 * tq), F32),
                        pltpu.VMEM((1, 2 * tq), F32),
                        pltpu.VMEM((VT_ROWS, 2 * tq), F32)],
        compiler_params=pltpu.CompilerParams(
            dimension_semantics=("parallel", "parallel", "arbitrary"),
            vmem_limit_bytes=VMEM_LIMIT_BYTES),
        name="gqa_attention",
    )(qt, k, vt)


def _diff_kernel(scal_ref, qt_ref, k_ref, vt_ref, lq_ref, lk_ref, gsub_ref, o_ref,
                 qs_scr, s_scr, max_scr, key_scr, m_scr, acc_scr):
    tq = qt_ref.shape[2]
    nk, _, tk = vt_ref.shape
    pair = pl.program_id(1)
    qi = pl.program_id(2)
    lam_init = scal_ref[0]
    slope_a = scal_ref[1 + 2 * pair]
    slope_b = scal_ref[2 + 2 * pair]
    nxt = _stage_query_tiles(qt_ref, qs_scr, qi, 4)
    _init_softmax_state(m_scr, acc_scr)

    q0 = qi * tq
    c_mid = q0 // tk
    col = lax.broadcasted_iota(jnp.int32, (1, 4 * tq), 1)
    col_slope = jnp.where(col < 2 * tq, slope_a, slope_b)
    col_pos = (q0 + col % tq).astype(F32)
    key_off = lax.broadcasted_iota(jnp.int32, (tk, 4 * tq), 0)

    tiles_per_chunk = tk // tq

    @pl.when(qi == 0)
    def _():
        key_table = key_off.astype(F32) * col_slope
        key_scr[0] = key_table
        key_scr[1] = -key_table
        rel = lax.broadcasted_iota(jnp.int32, (tk, 4 * tq), 1) % tq - key_off
        for t in range(tiles_per_chunk):
            key_scr[2 + t] = -(jnp.abs(rel + t * tq).astype(F32) * col_slope)

    def scores(ahead, c):
        tile = jnp.where(ahead == 0, qi, nxt)
        mid = tile // tiles_per_chunk
        table = jnp.where(c == mid, 2 + tile % tiles_per_chunk, jnp.where(c < mid, 0, 1))
        start = pl.multiple_of(c * tk, tk)
        s = jnp.dot(k_ref[pl.ds(start, tk), :], qs_scr[(qi + ahead) % 2],
                    preferred_element_type=F32)
        return s + key_scr[table]

    def consume(c, s, m_cur):
        sign = jnp.where(c == c_mid, 0.0, jnp.where(c < c_mid, 1.0, -1.0)).astype(F32)
        col_offset = col_slope * (sign * (col_pos - (c * tk).astype(F32)))
        vts = [vt_ref[c, :VT_ROWS, :], vt_ref[c, VT_ROWS:, :]]
        _online_softmax_step(s, m_cur, vts, m_scr, acc_scr, col_offset)

    _pipelined_chunks(nk, B_LOOKAHEAD, qi == 0, scores, consume, s_scr, max_scr)

    t = jnp.exp(jnp.sum(lq_ref[...] * lk_ref[...], axis=1, keepdims=True))
    lam = t[0:1, :] - t[1:2, :] + lam_init
    o = _normalized(acc_scr)
    o_a = o[:, 0:tq] - lam * o[:, tq:2 * tq]
    o_b = o[:, 2 * tq:3 * tq] - lam * o[:, 3 * tq:]
    o = jnp.concatenate([o_a, o_b], axis=0).T
    ss = _group_sum(o * o, HEAD_DIM)
    o = o * lax.rsqrt(ss * (1.0 / HEAD_DIM) + RMS_EPS) * gsub_ref[...] * (1.0 - lam_init)
    o_ref[...] = o.astype(o_ref.dtype)


def _diff_call(scal, qt, k, vt, lq, lk, gsub):
    b, nk, vt_rows, tk = vt.shape
    vt_rows //= B_HEADS // 2
    seq = k.shape[1]
    tq = min(B_Q_TILE, seq)
    assert tk % tq == 0
    qt_spec, k_spec, vt_spec, o_spec = _dense_specs(b, seq, tq, nk, vt_rows, tk)
    full = lambda a: pl.BlockSpec(a.shape, lambda bi, g, qi: (0,) * a.ndim)
    return pl.pallas_call(
        _diff_kernel,
        grid=(b, B_HEADS // 2, seq // tq),
        in_specs=[pl.BlockSpec(memory_space=pltpu.SMEM), qt_spec, k_spec, vt_spec,
                  full(lq), full(lk), full(gsub)],
        out_specs=o_spec,
        out_shape=jax.ShapeDtypeStruct((b, seq, MIXER_WIDTH), BF16),
        scratch_shapes=[pltpu.VMEM((2, LANES, 4 * tq), BF16),
                        pltpu.VMEM((SCORE_SLOTS, tk + SLOT_SKEW, 4 * tq), F32),
                        pltpu.VMEM((SCORE_SLOTS, 1, 4 * tq), F32),
                        pltpu.VMEM((2 + tk // tq, tk, 4 * tq), F32),
                        pltpu.VMEM((1, 4 * tq), F32),
                        pltpu.VMEM((VT_ROWS, 4 * tq), F32)],
        compiler_params=pltpu.CompilerParams(
            dimension_semantics=("parallel", "parallel", "arbitrary"),
            vmem_limit_bytes=VMEM_LIMIT_BYTES),
        name="diff_attention",
    )(scal, qt, k, vt, lq, lk, gsub)


def _nbr_kernel(q_ref, kp_ref, kc_ref, kn_ref, vp_ref, vc_ref, vn_ref, bias_ref, o_ref,
                kband, vband, *, grid_rows):
    blk = q_ref.shape[0]
    rb = blk // GRID_W
    i = pl.program_id(1)
    kband[0:blk] = kp_ref[...]
    kband[blk:2 * blk] = kc_ref[...]
    kband[2 * blk:] = kn_ref[...]
    vband[0:blk] = vp_ref[...]
    vband[blk:2 * blk] = vc_ref[...]
    vband[2 * blk:] = vn_ref[...]
    band = NA_ROWS * GRID_W
    lane_head = lax.broadcasted_iota(jnp.int32, (GRID_W, MIXER_WIDTH), 1) // HEAD_DIM
    for jr in range(rb):
        r = i * rb + jr
        rs = jnp.clip(r - NA_ROWS // 2, 0, grid_rows - NA_ROWS)
        start = pl.multiple_of((rs - (i - 1) * rb) * GRID_W, GRID_W)
        k = kband[pl.ds(start, band), :]
        v = vband[pl.ds(start, band), :]
        qs = _stack_masked(q_ref[jr * GRID_W:(jr + 1) * GRID_W, :], C_HEADS)
        s = lax.dot_general(qs, k, (((1,), (1,)), ((), ())), preferred_element_type=F32)
        s = s + bias_ref[r - rs].reshape(C_HEADS * GRID_W, band)
        m = jnp.max(s, axis=1, keepdims=True)
        p = jnp.exp2(s - m)
        l = jnp.sum(p, axis=1, keepdims=True)
        pv = jnp.dot(p.astype(BF16), v, preferred_element_type=F32) / l
        o = jnp.zeros((GRID_W, MIXER_WIDTH), F32)
        for hd in range(C_HEADS):
            o = jnp.where(lane_head == hd, pv[hd * GRID_W:(hd + 1) * GRID_W], o)
        o_ref[jr * GRID_W:(jr + 1) * GRID_W, :] = o.astype(o_ref.dtype)


def _nbr_call(q, k, v, bias):
    b, seq, w = q.shape
    grid_rows = seq // GRID_W
    rb = min(C_BLOCK_ROWS, grid_rows)
    blk = rb * GRID_W
    nblk = seq // blk
    cur = pl.BlockSpec((None, blk, w), lambda bi, i: (bi, i, 0))
    prev = pl.BlockSpec((None, blk, w), lambda bi, i: (bi, jnp.maximum(i - 1, 0), 0))
    nxt = pl.BlockSpec((None, blk, w), lambda bi, i: (bi, jnp.minimum(i + 1, nblk - 1), 0))
    return pl.pallas_call(
        functools.partial(_nbr_kernel, grid_rows=grid_rows),
        grid=(b, nblk),
        in_specs=[cur, prev, cur, nxt, prev, cur, nxt,
                  pl.BlockSpec(bias.shape, lambda bi, i: (0, 0, 0, 0))],
        out_specs=cur,
        out_shape=jax.ShapeDtypeStruct((b, seq, w), BF16),
        scratch_shapes=[pltpu.VMEM((3 * blk, w), BF16), pltpu.VMEM((3 * blk, w), BF16)],
        compiler_params=pltpu.CompilerParams(
            dimension_semantics=("parallel", "parallel"), vmem_limit_bytes=VMEM_LIMIT_BYTES),
        name="nbr_attention",
    )(q, k, k, k, v, v, v, bias)


def _nbr_bias_table(rpb):
    qc = np.arange(GRID_W)
    kc = np.arange(GRID_W)
    cs = np.clip(qc - NA_COLS // 2, 0, GRID_W - NA_COLS)
    valid = (kc[None, :] >= cs[:, None]) & (kc[None, :] < cs[:, None] + NA_COLS)
    dc = kc[None, :] - qc[:, None] + (NA_COLS - 1)
    d = np.arange(NA_ROWS)
    a = np.arange(NA_ROWS)
    dr = a[None, :] - d[:, None] + (NA_ROWS - 1)
    pick_r = (dr[:, :, None] == np.arange(2 * NA_ROWS - 1)).astype(np.float32)
    pick_c = ((dc[:, :, None] == np.arange(2 * NA_COLS - 1)) & valid[:, :, None]).astype(np.float32)
    tbl = jnp.einsum("hrc,dar,qkc->dhqak", rpb.astype(F32) * LOG2E, pick_r, pick_c,
                     precision=lax.Precision.HIGHEST)
    tbl = jnp.where(valid[None, None, :, None, :], tbl, NEG_BIG)
    return tbl.reshape(NA_ROWS, C_HEADS, GRID_W, NA_ROWS * GRID_W)


def _channel_kernel(h_ref, ya_ref, yb_ref, yc_ref, yd_ref, wout_ref, gffn_ref, wg_ref, wu_ref,
                    wd_ref, gple_ref, wpg_ref, p_ref, wpp_ref, gfin_ref, o_ref, *, final):
    w = MIXER_WIDTH
    h = h_ref[...]
    for j, y_ref in enumerate((ya_ref, yb_ref, yc_ref, yd_ref)):
        h = h + jnp.dot(y_ref[...], wout_ref[j * w:(j + 1) * w, :], preferred_element_type=F32)

    hn = _rms(h, gffn_ref[...]).astype(BF16)
    ffn = wg_ref.shape[1]
    acc = jnp.zeros(h.shape, F32)
    for c in range(ffn // FFN_CHUNK):
        cols = slice(c * FFN_CHUNK, (c + 1) * FFN_CHUNK)
        gate = jnp.dot(hn, wg_ref[:, cols], preferred_element_type=F32)
        up = jnp.dot(hn, wu_ref[:, cols], preferred_element_type=F32)
        act = (gate * jax.nn.sigmoid(gate) * up).astype(BF16)
        acc = acc + jnp.dot(act, wd_ref[cols, :], preferred_element_type=F32)
    h = h + acc

    hn = _rms(h, gple_ref[...]).astype(BF16)
    gate = jax.nn.sigmoid(jnp.dot(hn, wpg_ref[...], preferred_element_type=F32))
    emb = jnp.dot(p_ref[...].astype(BF16), wpp_ref[...], preferred_element_type=F32)
    h = h + gate * emb
    if final:
        h = _rms(h, gfin_ref[...])
    o_ref[...] = h


def _channel_call(h, ya, yb, yc, yd, wout, gffn, wg, wu, wd, gple, wpg, p, wpp, gfin, final, seq):
    rows, d = h.shape
    tm = min(ROW_TILE, seq)
    row_spec = lambda w: pl.BlockSpec((tm, w), lambda i: (i, 0))
    resident = lambda a: pl.BlockSpec(a.shape, lambda i: (0,) * a.ndim,
                                      pipeline_mode=pl.Buffered(1))
    return pl.pallas_call(
        functools.partial(_channel_kernel, final=final),
        grid=(rows // tm,),
        in_specs=[row_spec(d)] + [row_spec(MIXER_WIDTH)] * 4
                 + [resident(wout), resident(gffn), resident(wg), resident(wu), resident(wd),
                    resident(gple), resident(wpg), row_spec(p.shape[1]), resident(wpp),
                    resident(gfin)],
        out_specs=row_spec(d),
        out_shape=jax.ShapeDtypeStruct((rows, d), F32),
        compiler_params=pltpu.CompilerParams(
            dimension_semantics=("parallel",), vmem_limit_bytes=VMEM_LIMIT_BYTES),
        name="channel_mixers",
    )(h, ya, yb, yc, yd, wout, gffn, wg, wu, wd, gple, wpg, p, wpp, gfin)


def _rope_tables(seq):
    t = jnp.arange(seq)
    row = (t // GRID_W).astype(F32)
    col = (t % GRID_W).astype(F32)
    n_freq = HEAD_DIM // 4
    inv = ROPE_THETA ** (-jnp.arange(n_freq, dtype=F32) / n_freq)
    ang = jnp.concatenate([row[:, None] * inv, col[:, None] * inv], axis=-1)
    cos, sin = jnp.cos(ang), jnp.sin(ang)
    heads = LANES // HEAD_DIM
    return (jnp.tile(jnp.concatenate([cos, cos], axis=-1), (1, heads)),
            jnp.tile(jnp.concatenate([-sin, sin], axis=-1), (1, heads)))


def kernel(x, p, g_mix, w_in, a_q_norm, a_k_norm, b_lam_q, b_lam_k, b_sub_norm, c_rpb, d_ln_g,
           d_ln_b, d_w_s, d_b_s, w_out, g_ffn, w_gate, w_up, w_down, g_ple, w_ple_gate,
           w_ple_proj, g_final):
    b, seq, d = x.shape
    depth = w_in.shape[0]
    assert seq % GRID_W == 0 and seq // GRID_W >= NA_ROWS and seq % D_CHUNK == 0
    rows = b * seq
    h = x.reshape(rows, d)
    cos_t, sin_t = _rope_tables(seq)
    slopes = (2.0 ** (-8.0 / B_HEADS)) ** jnp.arange(1, B_HEADS + 1, dtype=F32) * LOG2E
    row2 = lambda a: a.reshape(1, -1).astype(F32)
    for i in range(depth):
        bs_tile = jnp.repeat(d_b_s[i].T, HEAD_DIM, axis=1)
        aqt, ak, avt, bqt, bk, bvt, cq, ck, cv, yd = _proj_call(
            h, row2(g_mix[i]), w_in[i].astype(BF16),
            row2(jnp.tile(a_q_norm[i], A_HEADS)), row2(jnp.tile(a_k_norm[i], A_KV_HEADS)),
            cos_t, sin_t, row2(d_ln_g[i]), row2(d_ln_b[i]), d_w_s[i].astype(BF16),
            bs_tile.astype(F32), seq)
        sh = lambda a: a.reshape(b, seq, MIXER_WIDTH)
        vt4 = lambda a: a.reshape(b, -1, a.shape[-2], a.shape[-1])
        ya = _gqa_call(vt4(aqt), sh(ak), vt4(avt))
        lam_init = 0.8 - 0.6 * math.exp(-0.3 * i)
        scal = jnp.concatenate([jnp.full((1,), lam_init, F32), slopes])
        yb = _diff_call(scal, vt4(bqt), sh(bk), vt4(bvt), b_lam_q[i].astype(F32),
                        b_lam_k[i].astype(F32), row2(jnp.tile(b_sub_norm[i], LANES // HEAD_DIM)))
        yc = _nbr_call(sh(cq), sh(ck), sh(cv), _nbr_bias_table(c_rpb[i]))
        fl = lambda a: a.reshape(rows, MIXER_WIDTH)
        h = _channel_call(
            h, fl(ya), fl(yb), fl(yc), yd, w_out[i].astype(BF16), row2(g_ffn[i]),
            w_gate[i].astype(BF16), w_up[i].astype(BF16), w_down[i].astype(BF16),
            row2(g_ple[i]), w_ple_gate[i].astype(BF16), p[i].reshape(rows, -1),
            w_ple_proj[i].astype(BF16), row2(g_final), i == depth - 1, seq)
    return h.reshape(b, seq, d)
```

```python
import functools
import math

import jax
import jax.numpy as jnp
import numpy as np
from jax import lax
from jax.experimental import pallas as pl
from jax.experimental.pallas import tpu as pltpu

F32 = jnp.float32
BF16 = jnp.bfloat16

HEAD_DIM = 64
GRID_W = 64
RMS_EPS = 1e-6
LN_EPS = 1e-5
A_HEADS = 4
A_KV_HEADS = 2
ROPE_THETA = 10000.0
B_HEADS = 4
B_QK_DIM = HEAD_DIM // 2
C_HEADS = 4
NA_ROWS = 8
NA_COLS = 16
D_GROUPS = 4
D_CHUNK = 128
MIXER_WIDTH = 256

LOG2E = math.log2(math.e)
NEG_BIG = -1e30

LANES = 128
BF16_SUBLANES = 16
VMEM_LIMIT_BYTES = 56 * 1024 * 1024

VT_ROWS = HEAD_DIM + BF16_SUBLANES

ROW_TILE = 512
A_Q_TILE = 256
B_Q_TILE = 128
C_BLOCK_ROWS = 8
FFN_CHUNK = 256
PIPELINE_UNROLL = 16
SCORE_SLOTS = 4
A_LOOKAHEAD = 2
B_LOOKAHEAD = 1


def _group_ones(width, group):
    r = lax.broadcasted_iota(jnp.int32, (width, width), 0) // group
    c = lax.broadcasted_iota(jnp.int32, (width, width), 1) // group
    return (r == c).astype(F32)


def _group_sum(x, group):
    return jnp.dot(x, _group_ones(x.shape[-1], group), precision=lax.Precision.HIGHEST,
                   preferred_element_type=F32)


def _rms(x, g):
    ms = jnp.mean(x * x, axis=-1, keepdims=True)
    return x * lax.rsqrt(ms + RMS_EPS) * g


def _proj_kernel(h_ref, gmix_ref, win_ref, gq_ref, gk_ref, cos_ref, sin_ref,
                 lng_ref, lnb_ref, ws_ref, bs_ref,
                 aqt_o, ak_o, avt_o, bqt_o, bk_o, bvt_o, cq_o, ck_o, cv_o, yd_o):
    tm = h_ref.shape[0]
    hn = _rms(h_ref[...], gmix_ref[...]).astype(BF16)

    def proj(lo, width):
        return jnp.dot(hn, win_ref[:, lo:lo + width], preferred_element_type=F32)

    cos = cos_ref[...]
    sin = sin_ref[...]

    def qk_norm_rope(y, gain, cos_t, sin_t):
        w = y.shape[-1]
        ss = _group_sum(y * y, HEAD_DIM)
        y = y * lax.rsqrt(ss * (1.0 / HEAD_DIM) + RMS_EPS) * gain
        lane = lax.broadcasted_iota(jnp.int32, y.shape, 1)
        half = HEAD_DIM // 2
        swapped = jnp.where(lane % HEAD_DIM < half,
                            pltpu.roll(y, w - half, axis=1), pltpu.roll(y, half, axis=1))
        return y * cos_t + swapped * sin_t

    a_scale = HEAD_DIM ** -0.5 * LOG2E
    cos2 = jnp.concatenate([cos, cos], axis=1)
    sin2 = jnp.concatenate([sin, sin], axis=1)
    aq = qk_norm_rope(proj(0, 256), gq_ref[...], cos2, sin2) * a_scale
    def store_tiles(o_ref, xt):
        tq = o_ref.shape[-1]
        for j in range(o_ref.shape[0]):
            o_ref[j] = xt[:, j * tq:(j + 1) * tq].astype(BF16)

    store_tiles(aqt_o, aq.T)
    ak = qk_norm_rope(proj(256, 128), gk_ref[...], cos, sin)
    av = proj(384, 128)
    lane128 = lax.broadcasted_iota(jnp.int32, (tm, LANES), 1)
    first = lane128 < HEAD_DIM

    def dup_heads(y):
        r = pltpu.roll(y, HEAD_DIM, axis=1)
        return jnp.concatenate([jnp.where(first, y, r), jnp.where(first, r, y)], axis=1)

    ak_o[...] = dup_heads(ak).astype(BF16)
    ones = jnp.ones((BF16_SUBLANES, tm), F32)
    avt = av.T
    avt_o[...] = jnp.concatenate(
        [avt[:HEAD_DIM], ones, avt[HEAD_DIM:], ones], axis=0).astype(BF16)

    store_tiles(bqt_o, (proj(512, 256) * (B_QK_DIM ** -0.5 * LOG2E)).T)
    bk_o[...] = proj(768, 256).astype(BF16)
    bvt = proj(1024, 256).T
    bvt_o[...] = jnp.concatenate(
        [x for hd in range(B_HEADS) for x in (bvt[hd * HEAD_DIM:(hd + 1) * HEAD_DIM], ones)],
        axis=0).astype(BF16)

    cq_o[...] = (proj(1280, 256) * a_scale).astype(BF16)
    ck_o[...] = proj(1536, 256).astype(BF16)
    cv_o[...] = proj(1792, 256).astype(BF16)

    def gelu(t):
        return 0.5 * t * (1.0 + lax.erf(t * (2.0 ** -0.5)))

    u = gelu(proj(2048, 256))
    vv = gelu(proj(2304, 256))
    mu = jnp.mean(vv, axis=-1, keepdims=True)
    vc = vv - mu
    var = jnp.mean(vc * vc, axis=-1, keepdims=True)
    vv = (vc * lax.rsqrt(var + LN_EPS) * lng_ref[...] + lnb_ref[...]).astype(BF16)
    lane_group = lax.broadcasted_iota(jnp.int32, (D_CHUNK, MIXER_WIDTH), 1) // HEAD_DIM
    for n in range(tm // D_CHUNK):
        rows = slice(n * D_CHUNK, (n + 1) * D_CHUNK)
        vchunk = vv[rows, :]
        sv = bs_ref[...]
        for g in range(D_GROUPS):
            mixed = jnp.dot(ws_ref[g], vchunk, preferred_element_type=F32)
            sv = sv + jnp.where(lane_group == g, mixed, 0.0)
        yd_o[rows, :] = (u[rows, :] * sv).astype(BF16)


def _proj_call(h, gmix, win, gq, gk, cos_t, sin_t, lng, lnb, ws, bs, seq):
    rows, d = h.shape
    tm = min(ROW_TILE, seq)
    steps_per_seq = seq // tm
    row_spec = lambda w: pl.BlockSpec((tm, w), lambda i: (i, 0))
    full = lambda a: pl.BlockSpec(a.shape, lambda i: (0,) * a.ndim)
    tab_spec = pl.BlockSpec((tm, LANES), lambda i: (i % steps_per_seq, 0))
    row_out = (row_spec(MIXER_WIDTH), jax.ShapeDtypeStruct((rows, MIXER_WIDTH), BF16))
    def qt_out(tile):
        tq = min(tile, seq)
        return (pl.BlockSpec((tm // tq, MIXER_WIDTH, tq), lambda i: (i, 0, 0)),
                jax.ShapeDtypeStruct((rows // tq, MIXER_WIDTH, tq), BF16))
    def vt_out(heads):
        r = heads * VT_ROWS
        return (pl.BlockSpec((None, r, tm), lambda i: (i, 0, 0)),
                jax.ShapeDtypeStruct((rows // tm, r, tm), BF16))
    outs = ([qt_out(A_Q_TILE), row_out, vt_out(A_KV_HEADS),
             qt_out(B_Q_TILE), row_out, vt_out(B_HEADS)] + [row_out] * 4)
    return pl.pallas_call(
        _proj_kernel,
        grid=(rows // tm,),
        in_specs=[row_spec(d), full(gmix), full(win), full(gq), full(gk), tab_spec, tab_spec,
                  full(lng), full(lnb), full(ws), full(bs)],
        out_specs=[o[0] for o in outs],
        out_shape=[o[1] for o in outs],
        compiler_params=pltpu.CompilerParams(
            dimension_semantics=("parallel",), vmem_limit_bytes=VMEM_LIMIT_BYTES),
        name="proj_mixers",
    )(h, gmix, win, gq, gk, cos_t, sin_t, lng, lnb, ws, bs)


def _online_softmax_step(s, m_cur, vts, m_scr, acc_scr, col_offset=None):
    m_prev = m_scr[...]
    if col_offset is not None:
        m_cur = m_cur - col_offset
    m_next = jnp.maximum(m_prev, m_cur)
    alpha = jnp.exp2(m_prev - m_next)
    shift = m_next if col_offset is None else m_next + col_offset
    p = jnp.exp2(s - shift).astype(BF16)
    width = p.shape[1] // len(vts)
    pv = [jnp.dot(vt, p[:, j * width:(j + 1) * width], preferred_element_type=F32)
          for j, vt in enumerate(vts)]
    acc_scr[...] = alpha * acc_scr[...] + jnp.concatenate(pv, axis=1)
    m_scr[...] = m_next


def _stack_masked_t(qt, groups):
    qt = qt.astype(F32)
    width = qt.shape[0] // groups
    row = lax.broadcasted_iota(jnp.int32, qt.shape, 0) // width
    zero = jnp.zeros_like(qt)
    return jnp.concatenate([jnp.where(row == j, qt, zero) for j in range(groups)],
                           axis=1).astype(BF16)


def _stack_masked(q, groups):
    width = q.shape[1] // groups
    lane = lax.broadcasted_iota(jnp.int32, q.shape, 1) // width
    zero = jnp.zeros_like(q)
    return jnp.concatenate([jnp.where(lane == j, q, zero) for j in range(groups)], axis=0)


def _init_softmax_state(m_scr, acc_scr):
    m_scr[...] = jnp.full(m_scr.shape, NEG_BIG, F32)
    acc_scr[...] = jnp.zeros(acc_scr.shape, F32)


def _normalized(acc_scr):
    return acc_scr[:HEAD_DIM, :] / acc_scr[HEAD_DIM:HEAD_DIM + 1, :]


def _dense_specs(b, seq, tq, nk, vt_rows, tk):
    qt_spec = pl.BlockSpec((None, seq // tq, LANES, tq), lambda bi, g, qi: (bi, 0, g, 0))
    k_spec = pl.BlockSpec((None, seq, LANES), lambda bi, g, qi: (bi, 0, g))
    vt_spec = pl.BlockSpec((None, nk, vt_rows, tk), lambda bi, g, qi: (bi, 0, g, 0))
    o_spec = pl.BlockSpec((None, tq, LANES), lambda bi, g, qi: (bi, qi, g))
    return qt_spec, k_spec, vt_spec, o_spec


def _pipelined_chunks(nk, lookahead, first_tile, scores, consume, s_scr, max_scr):
    unroll = PIPELINE_UNROLL if nk % PIPELINE_UNROLL == 0 else s_scr.shape[0]
    slots = s_scr.shape[0]
    assert nk % unroll == 0 and unroll % slots == 0 and lookahead < slots <= nk

    def produce(ahead, c, slot):
        s = scores(ahead, c)
        s_scr[slot] = s
        max_scr[slot] = jnp.max(s, axis=0, keepdims=True)

    @pl.when(first_tile)
    def _():
        for c in range(lookahead):
            produce(0, c, c)

    def body(i, carry):
        for j in range(unroll):
            c = unroll * i + j
            ahead = c + lookahead
            if j + lookahead < unroll:
                produce(0, ahead, (j + lookahead) % slots)
            else:
                wrap = (ahead >= nk).astype(jnp.int32)
                produce(wrap, ahead - wrap * nk, (j + lookahead) % slots)
            consume(c, s_scr[j % slots], max_scr[j % slots])
        return carry

    lax.fori_loop(0, nk // unroll, body, 0)


def _stage_query_tiles(qt_ref, qs_scr, qi, groups):
    last = qt_ref.shape[0] - 1

    @pl.when(qi == 0)
    def _():
        qs_scr[0] = _stack_masked_t(qt_ref[0], groups)

    nxt = jnp.minimum(qi + 1, last)
    qs_scr[(qi + 1) % 2] = _stack_masked_t(qt_ref[nxt], groups)
    return nxt


def _gqa_kernel(qt_ref, k_ref, vt_ref, o_ref, qs_scr, s_scr, max_scr, m_scr, acc_scr):
    tq = qt_ref.shape[2]
    nk, _, tk = vt_ref.shape
    qi = pl.program_id(2)
    _stage_query_tiles(qt_ref, qs_scr, qi, 2)
    _init_softmax_state(m_scr, acc_scr)

    def scores(ahead, c):
        start = pl.multiple_of(c * tk, tk)
        return jnp.dot(k_ref[pl.ds(start, tk), :], qs_scr[(qi + ahead) % 2],
                       preferred_element_type=F32)

    def consume(c, s, m_cur):
        _online_softmax_step(s, m_cur, [vt_ref[c]], m_scr, acc_scr)

    _pipelined_chunks(nk, A_LOOKAHEAD, qi == 0, scores, consume, s_scr, max_scr)
    o = _normalized(acc_scr)
    o_ref[...] = jnp.concatenate([o[:, :tq], o[:, tq:]], axis=0).T.astype(o_ref.dtype)


def _gqa_call(qt, k, vt):
    b, nk, vt_rows, tk = vt.shape
    vt_rows //= A_KV_HEADS
    seq = k.shape[1]
    tq = min(A_Q_TILE, seq)
    qt_spec, k_spec, vt_spec, o_spec = _dense_specs(b, seq, tq, nk, vt_rows, tk)
    return pl.pallas_call(
        _gqa_kernel,
        grid=(b, A_KV_HEADS, seq // tq),
        in_specs=[qt_spec, k_spec, vt_spec],
        out_specs=o_spec,
        out_shape=jax.ShapeDtypeStruct((b, seq, MIXER_WIDTH), BF16),
        scratch_shapes=[pltpu.VMEM((2, LANES, 2 * tq), BF16),
                        pltpu.VMEM((SCORE_SLOTS, tk, 2 * tq), F32),
                        pltpu.VMEM((SCORE_SLOTS, 1, 2 * tq), F32),
                        pltpu.VMEM((1, 2 * tq), F32),
                        pltpu.VMEM((VT_ROWS, 2 * tq), F32)],
        compiler_params=pltpu.CompilerParams(
            dimension_semantics=("parallel", "parallel", "arbitrary"),
            vmem_limit_bytes=VMEM_LIMIT_BYTES),
        name="gqa_attention",
    )(qt, k, vt)


def _diff_kernel(scal_ref, qt_ref, k_ref, vt_ref, lq_ref, lk_ref, gsub_ref, o_ref,
                 qs_scr, s_scr, max_scr, key_scr, m_scr, acc_scr):
    tq = qt_ref.shape[2]
    nk, _, tk = vt_ref.shape
    pair = pl.program_id(1)
    qi = pl.program_id(2)
    lam_init = scal_ref[0]
    slope_a = scal_ref[1 + 2 * pair]
    slope_b = scal_ref[2 + 2 * pair]
    nxt = _stage_query_tiles(qt_ref, qs_scr, qi, 4)
    _init_softmax_state(m_scr, acc_scr)

    q0 = qi * tq
    c_mid = q0 // tk
    col = lax.broadcasted_iota(jnp.int32, (1, 4 * tq), 1)
    col_slope = jnp.where(col < 2 * tq, slope_a, slope_b)
    col_pos = (q0 + col % tq).astype(F32)
    key_off = lax.broadcasted_iota(jnp.int32, (tk, 4 * tq), 0)

    tiles_per_chunk = tk // tq

    @pl.when(qi == 0)
    def _():
        key_table = key_off.astype(F32) * col_slope
        key_scr[0] = key_table
        key_scr[1] = -key_table
        rel = lax.broadcasted_iota(jnp.int32, (tk, 4 * tq), 1) % tq - key_off
        for t in range(tiles_per_chunk):
            key_scr[2 + t] = -(jnp.abs(rel + t * tq).astype(F32) * col_slope)

    def scores(ahead, c):
        tile = jnp.where(ahead == 0, qi, nxt)
        mid = tile // tiles_per_chunk
        table = jnp.where(c == mid, 2 + tile % tiles_per_chunk, jnp.where(c < mid, 0, 1))
        start = pl.multiple_of(c * tk, tk)
        s = jnp.dot(k_ref[pl.ds(start, tk), :], qs_scr[(qi + ahead) % 2],
                    preferred_element_type=F32)
        return s + key_scr[table]

    def consume(c, s, m_cur):
        sign = jnp.where(c == c_mid, 0.0, jnp.where(c < c_mid, 1.0, -1.0)).astype(F32)
        col_offset = col_slope * (sign * (col_pos - (c * tk).astype(F32)))
        vts = [vt_ref[c, :VT_ROWS, :], vt_ref[c, VT_ROWS:, :]]
        _online_softmax_step(s, m_cur, vts, m_scr, acc_scr, col_offset)

    _pipelined_chunks(nk, B_LOOKAHEAD, qi == 0, scores, consume, s_scr, max_scr)

    t = jnp.exp(jnp.sum(lq_ref[...] * lk_ref[...], axis=1, keepdims=True))
    lam = t[0:1, :] - t[1:2, :] + lam_init
    o = _normalized(acc_scr)
    o_a = o[:, 0:tq] - lam * o[:, tq:2 * tq]
    o_b = o[:, 2 * tq:3 * tq] - lam * o[:, 3 * tq:]
    def head_rms(y):
        return y * lax.rsqrt(jnp.mean(y * y, axis=0, keepdims=True) + RMS_EPS)

    o = jnp.concatenate([head_rms(o_a), head_rms(o_b)], axis=0).T
    o_ref[...] = (o * gsub_ref[...] * (1.0 - lam_init)).astype(o_ref.dtype)


def _diff_call(scal, qt, k, vt, lq, lk, gsub):
    b, nk, vt_rows, tk = vt.shape
    vt_rows //= B_HEADS // 2
    seq = k.shape[1]
    tq = min(B_Q_TILE, seq)
    assert tk % tq == 0
    qt_spec, k_spec, vt_spec, o_spec = _dense_specs(b, seq, tq, nk, vt_rows, tk)
    full = lambda a: pl.BlockSpec(a.shape, lambda bi, g, qi: (0,) * a.ndim)
    return pl.pallas_call(
        _diff_kernel,
        grid=(b, B_HEADS // 2, seq // tq),
        in_specs=[pl.BlockSpec(memory_space=pltpu.SMEM), qt_spec, k_spec, vt_spec,
                  full(lq), full(lk), full(gsub)],
        out_specs=o_spec,
        out_shape=jax.ShapeDtypeStruct((b, seq, MIXER_WIDTH), BF16),
        scratch_shapes=[pltpu.VMEM((2, LANES, 4 * tq), BF16),
                        pltpu.VMEM((SCORE_SLOTS, tk, 4 * tq), F32),
                        pltpu.VMEM((SCORE_SLOTS, 1, 4 * tq), F32),
                        pltpu.VMEM((2 + tk // tq, tk, 4 * tq), F32),
                        pltpu.VMEM((1, 4 * tq), F32),
                        pltpu.VMEM((VT_ROWS, 4 * tq), F32)],
        compiler_params=pltpu.CompilerParams(
            dimension_semantics=("parallel", "parallel", "arbitrary"),
            vmem_limit_bytes=VMEM_LIMIT_BYTES),
        name="diff_attention",
    )(scal, qt, k, vt, lq, lk, gsub)


def _nbr_kernel(q_ref, kp_ref, kc_ref, kn_ref, vp_ref, vc_ref, vn_ref, bias_ref, o_ref,
                kband, vband, *, grid_rows):
    blk = q_ref.shape[0]
    rb = blk // GRID_W
    i = pl.program_id(1)
    kband[0:blk] = kp_ref[...]
    kband[blk:2 * blk] = kc_ref[...]
    kband[2 * blk:] = kn_ref[...]
    vband[0:blk] = vp_ref[...]
    vband[blk:2 * blk] = vc_ref[...]
    vband[2 * blk:] = vn_ref[...]
    band = NA_ROWS * GRID_W
    lane_head = lax.broadcasted_iota(jnp.int32, (GRID_W, MIXER_WIDTH), 1) // HEAD_DIM
    for jr in range(rb):
        r = i * rb + jr
        rs = jnp.clip(r - NA_ROWS // 2, 0, grid_rows - NA_ROWS)
        start = pl.multiple_of((rs - (i - 1) * rb) * GRID_W, GRID_W)
        k = kband[pl.ds(start, band), :]
        v = vband[pl.ds(start, band), :]
        qs = _stack_masked(q_ref[jr * GRID_W:(jr + 1) * GRID_W, :], C_HEADS)
        s = lax.dot_general(qs, k, (((1,), (1,)), ((), ())), preferred_element_type=F32)
        s = s + bias_ref[r - rs].reshape(C_HEADS * GRID_W, band)
        m = jnp.max(s, axis=1, keepdims=True)
        p = jnp.exp2(s - m)
        l = jnp.sum(p, axis=1, keepdims=True)
        pv = jnp.dot(p.astype(BF16), v, preferred_element_type=F32) / l
        o = jnp.zeros((GRID_W, MIXER_WIDTH), F32)
        for hd in range(C_HEADS):
            o = jnp.where(lane_head == hd, pv[hd * GRID_W:(hd + 1) * GRID_W], o)
        o_ref[jr * GRID_W:(jr + 1) * GRID_W, :] = o.astype(o_ref.dtype)


def _nbr_call(q, k, v, bias):
    b, seq, w = q.shape
    grid_rows = seq // GRID_W
    rb = min(C_BLOCK_ROWS, grid_rows)
    blk = rb * GRID_W
    nblk = seq // blk
    cur = pl.BlockSpec((None, blk, w), lambda bi, i: (bi, i, 0))
    prev = pl.BlockSpec((None, blk, w), lambda bi, i: (bi, jnp.maximum(i - 1, 0), 0))
    nxt = pl.BlockSpec((None, blk, w), lambda bi, i: (bi, jnp.minimum(i + 1, nblk - 1), 0))
    return pl.pallas_call(
        functools.partial(_nbr_kernel, grid_rows=grid_rows),
        grid=(b, nblk),
        in_specs=[cur, prev, cur, nxt, prev, cur, nxt,
                  pl.BlockSpec(bias.shape, lambda bi, i: (0, 0, 0, 0))],
        out_specs=cur,
        out_shape=jax.ShapeDtypeStruct((b, seq, w), BF16),
        scratch_shapes=[pltpu.VMEM((3 * blk, w), BF16), pltpu.VMEM((3 * blk, w), BF16)],
        compiler_params=pltpu.CompilerParams(
            dimension_semantics=("parallel", "parallel"), vmem_limit_bytes=VMEM_LIMIT_BYTES),
        name="nbr_attention",
    )(q, k, k, k, v, v, v, bias)


def _nbr_bias_table(rpb):
    qc = np.arange(GRID_W)
    kc = np.arange(GRID_W)
    cs = np.clip(qc - NA_COLS // 2, 0, GRID_W - NA_COLS)
    valid = (kc[None, :] >= cs[:, None]) & (kc[None, :] < cs[:, None] + NA_COLS)
    dc = kc[None, :] - qc[:, None] + (NA_COLS - 1)
    d = np.arange(NA_ROWS)
    a = np.arange(NA_ROWS)
    dr = a[None, :] - d[:, None] + (NA_ROWS - 1)
    pick_r = (dr[:, :, None] == np.arange(2 * NA_ROWS - 1)).astype(np.float32)
    pick_c = ((dc[:, :, None] == np.arange(2 * NA_COLS - 1)) & valid[:, :, None]).astype(np.float32)
    tbl = jnp.einsum("hrc,dar,qkc->dhqak", rpb.astype(F32) * LOG2E, pick_r, pick_c,
                     precision=lax.Precision.HIGHEST)
    tbl = jnp.where(valid[None, None, :, None, :], tbl, NEG_BIG)
    return tbl.reshape(NA_ROWS, C_HEADS, GRID_W, NA_ROWS * GRID_W)


def _channel_kernel(h_ref, ya_ref, yb_ref, yc_ref, yd_ref, wout_ref, gffn_ref, wg_ref, wu_ref,
                    wd_ref, gple_ref, wpg_ref, p_ref, wpp_ref, gfin_ref, o_ref, *, final):
    w = MIXER_WIDTH
    h = h_ref[...]
    for j, y_ref in enumerate((ya_ref, yb_ref, yc_ref, yd_ref)):
        h = h + jnp.dot(y_ref[...], wout_ref[j * w:(j + 1) * w, :], preferred_element_type=F32)

    hn = _rms(h, gffn_ref[...]).astype(BF16)
    ffn = wg_ref.shape[1]
    acc = jnp.zeros(h.shape, F32)
    for c in range(ffn // FFN_CHUNK):
        cols = slice(c * FFN_CHUNK, (c + 1) * FFN_CHUNK)
        gate = jnp.dot(hn, wg_ref[:, cols], preferred_element_type=F32)
        up = jnp.dot(hn, wu_ref[:, cols], preferred_element_type=F32)
        act = (gate * jax.nn.sigmoid(gate) * up).astype(BF16)
        acc = acc + jnp.dot(act, wd_ref[cols, :], preferred_element_type=F32)
    h = h + acc

    hn = _rms(h, gple_ref[...]).astype(BF16)
    gate = jax.nn.sigmoid(jnp.dot(hn, wpg_ref[...], preferred_element_type=F32))
    emb = jnp.dot(p_ref[...].astype(BF16), wpp_ref[...], preferred_element_type=F32)
    h = h + gate * emb
    if final:
        h = _rms(h, gfin_ref[...])
    o_ref[...] = h


def _channel_call(h, ya, yb, yc, yd, wout, gffn, wg, wu, wd, gple, wpg, p, wpp, gfin, final, seq):
    rows, d = h.shape
    tm = min(ROW_TILE, seq)
    row_spec = lambda w: pl.BlockSpec((tm, w), lambda i: (i, 0))
    resident = lambda a: pl.BlockSpec(a.shape, lambda i: (0,) * a.ndim,
                                      pipeline_mode=pl.Buffered(1))
    return pl.pallas_call(
        functools.partial(_channel_kernel, final=final),
        grid=(rows // tm,),
        in_specs=[row_spec(d)] + [row_spec(MIXER_WIDTH)] * 4
                 + [resident(wout), resident(gffn), resident(wg), resident(wu), resident(wd),
                    resident(gple), resident(wpg), row_spec(p.shape[1]), resident(wpp),
                    resident(gfin)],
        out_specs=row_spec(d),
        out_shape=jax.ShapeDtypeStruct((rows, d), F32),
        compiler_params=pltpu.CompilerParams(
            dimension_semantics=("parallel",), vmem_limit_bytes=VMEM_LIMIT_BYTES),
        name="channel_mixers",
    )(h, ya, yb, yc, yd, wout, gffn, wg, wu, wd, gple, wpg, p, wpp, gfin)


def _rope_tables(seq):
    t = jnp.arange(seq)
    row = (t // GRID_W).astype(F32)
    col = (t % GRID_W).astype(F32)
    n_freq = HEAD_DIM // 4
    inv = ROPE_THETA ** (-jnp.arange(n_freq, dtype=F32) / n_freq)
    ang = jnp.concatenate([row[:, None] * inv, col[:, None] * inv], axis=-1)
    cos, sin = jnp.cos(ang), jnp.sin(ang)
    heads = LANES // HEAD_DIM
    return (jnp.tile(jnp.concatenate([cos, cos], axis=-1), (1, heads)),
            jnp.tile(jnp.concatenate([-sin, sin], axis=-1), (1, heads)))


def kernel(x, p, g_mix, w_in, a_q_norm, a_k_norm, b_lam_q, b_lam_k, b_sub_norm, c_rpb, d_ln_g,
           d_ln_b, d_w_s, d_b_s, w_out, g_ffn, w_gate, w_up, w_down, g_ple, w_ple_gate,
           w_ple_proj, g_final):
    b, seq, d = x.shape
    depth = w_in.shape[0]
    assert seq % GRID_W == 0 and seq // GRID_W >= NA_ROWS and seq % D_CHUNK == 0
    rows = b * seq
    h = x.reshape(rows, d)
    cos_t, sin_t = _rope_tables(seq)
    slopes = (2.0 ** (-8.0 / B_HEADS)) ** jnp.arange(1, B_HEADS + 1, dtype=F32) * LOG2E
    row2 = lambda a: a.reshape(1, -1).astype(F32)
    for i in range(depth):
        bs_tile = jnp.repeat(d_b_s[i].T, HEAD_DIM, axis=1)
        aqt, ak, avt, bqt, bk, bvt, cq, ck, cv, yd = _proj_call(
            h, row2(g_mix[i]), w_in[i].astype(BF16),
            row2(jnp.tile(a_q_norm[i], A_HEADS)), row2(jnp.tile(a_k_norm[i], A_KV_HEADS)),
            cos_t, sin_t, row2(d_ln_g[i]), row2(d_ln_b[i]), d_w_s[i].astype(BF16),
            bs_tile.astype(F32), seq)
        sh = lambda a: a.reshape(b, seq, MIXER_WIDTH)
        vt4 = lambda a: a.reshape(b, -1, a.shape[-2], a.shape[-1])
        ya = _gqa_call(vt4(aqt), sh(ak), vt4(avt))
        lam_init = 0.8 - 0.6 * math.exp(-0.3 * i)
        scal = jnp.concatenate([jnp.full((1,), lam_init, F32), slopes])
        yb = _diff_call(scal, vt4(bqt), sh(bk), vt4(bvt), b_lam_q[i].astype(F32),
                        b_lam_k[i].astype(F32), row2(jnp.tile(b_sub_norm[i], LANES // HEAD_DIM)))
        yc = _nbr_call(sh(cq), sh(ck), sh(cv), _nbr_bias_table(c_rpb[i]))
        fl = lambda a: a.reshape(rows, MIXER_WIDTH)
        h = _channel_call(
            h, fl(ya), fl(yb), fl(yc), yd, w_out[i].astype(BF16), row2(g_ffn[i]),
            w_gate[i].astype(BF16), w_up[i].astype(BF16), w_down[i].astype(BF16),
            row2(g_ple[i]), w_ple_gate[i].astype(BF16), p[i].reshape(rows, -1),
            w_ple_proj[i].astype(BF16), row2(g_final), i == depth - 1, seq)
    return h.reshape(b, seq, d)
```

```python
import functools
import math

import jax
import jax.numpy as jnp
import numpy as np
from jax import lax
from jax.experimental import pallas as pl
from jax.experimental.pallas import tpu as pltpu

F32 = jnp.float32
BF16 = jnp.bfloat16

HEAD_DIM = 64
GRID_W = 64
RMS_EPS = 1e-6
LN_EPS = 1e-5
A_HEADS = 4
A_KV_HEADS = 2
ROPE_THETA = 10000.0
B_HEADS = 4
B_QK_DIM = HEAD_DIM // 2
C_HEADS = 4
NA_ROWS = 8
NA_COLS = 16
D_GROUPS = 4
D_CHUNK = 128
MIXER_WIDTH = 256

LOG2E = math.log2(math.e)
NEG_BIG = -1e30

LANES = 128
BF16_SUBLANES = 16
VMEM_LIMIT_BYTES = 56 * 1024 * 1024

VT_ROWS = HEAD_DIM + BF16_SUBLANES

ROW_TILE = 512
A_Q_TILE = 256
B_Q_TILE = 128
C_BLOCK_ROWS = 8
FFN_CHUNK = 256
PIPELINE_UNROLL = 32
SCORE_SLOTS = 4
A_LOOKAHEAD = 2
B_LOOKAHEAD = 1


def _group_ones(width, group):
    r = lax.broadcasted_iota(jnp.int32, (width, width), 0) // group
    c = lax.broadcasted_iota(jnp.int32, (width, width), 1) // group
    return (r == c).astype(F32)


def _group_sum(x, group):
    return jnp.dot(x, _group_ones(x.shape[-1], group), precision=lax.Precision.HIGHEST,
                   preferred_element_type=F32)


def _rms(x, g):
    ms = jnp.mean(x * x, axis=-1, keepdims=True)
    return x * lax.rsqrt(ms + RMS_EPS) * g


def _proj_kernel(h_ref, gmix_ref, win_ref, gq_ref, gk_ref, cos_ref, sin_ref,
                 lng_ref, lnb_ref, ws_ref, bs_ref,
                 aqt_o, ak_o, avt_o, bqt_o, bk_o, bvt_o, cq_o, ck_o, cv_o, yd_o):
    tm = h_ref.shape[0]
    hn = _rms(h_ref[...], gmix_ref[...]).astype(BF16)

    def proj(lo, width):
        return jnp.dot(hn, win_ref[:, lo:lo + width], preferred_element_type=F32)

    cos = cos_ref[...]
    sin = sin_ref[...]

    def qk_norm_rope(y, gain, cos_t, sin_t):
        w = y.shape[-1]
        ss = _group_sum(y * y, HEAD_DIM)
        y = y * lax.rsqrt(ss * (1.0 / HEAD_DIM) + RMS_EPS) * gain
        lane = lax.broadcasted_iota(jnp.int32, y.shape, 1)
        half = HEAD_DIM // 2
        swapped = jnp.where(lane % HEAD_DIM < half,
                            pltpu.roll(y, w - half, axis=1), pltpu.roll(y, half, axis=1))
        return y * cos_t + swapped * sin_t

    a_scale = HEAD_DIM ** -0.5 * LOG2E
    cos2 = jnp.concatenate([cos, cos], axis=1)
    sin2 = jnp.concatenate([sin, sin], axis=1)
    aq = qk_norm_rope(proj(0, 256), gq_ref[...], cos2, sin2) * a_scale
    def store_tiles(o_ref, xt):
        tq = o_ref.shape[-1]
        for j in range(o_ref.shape[0]):
            o_ref[j] = xt[:, j * tq:(j + 1) * tq].astype(BF16)

    store_tiles(aqt_o, aq.T)
    ak = qk_norm_rope(proj(256, 128), gk_ref[...], cos, sin)
    av = proj(384, 128)
    lane128 = lax.broadcasted_iota(jnp.int32, (tm, LANES), 1)
    first = lane128 < HEAD_DIM

    def dup_heads(y):
        r = pltpu.roll(y, HEAD_DIM, axis=1)
        return jnp.concatenate([jnp.where(first, y, r), jnp.where(first, r, y)], axis=1)

    ak_o[...] = dup_heads(ak).astype(BF16)
    ones = jnp.ones((BF16_SUBLANES, tm), F32)
    avt = av.T
    avt_o[...] = jnp.concatenate(
        [avt[:HEAD_DIM], ones, avt[HEAD_DIM:], ones], axis=0).astype(BF16)

    store_tiles(bqt_o, (proj(512, 256) * (B_QK_DIM ** -0.5 * LOG2E)).T)
    bk_o[...] = proj(768, 256).astype(BF16)
    bvt = proj(1024, 256).T
    bvt_o[...] = jnp.concatenate(
        [x for hd in range(B_HEADS) for x in (bvt[hd * HEAD_DIM:(hd + 1) * HEAD_DIM], ones)],
        axis=0).astype(BF16)

    cq_o[...] = (proj(1280, 256) * a_scale).astype(BF16)
    ck_o[...] = proj(1536, 256).astype(BF16)
    cv_o[...] = proj(1792, 256).astype(BF16)

    def gelu(t):
        return 0.5 * t * (1.0 + lax.erf(t * (2.0 ** -0.5)))

    u = gelu(proj(2048, 256))
    vv = gelu(proj(2304, 256))
    mu = jnp.mean(vv, axis=-1, keepdims=True)
    vc = vv - mu
    var = jnp.mean(vc * vc, axis=-1, keepdims=True)
    vv = (vc * lax.rsqrt(var + LN_EPS) * lng_ref[...] + lnb_ref[...]).astype(BF16)
    lane_group = lax.broadcasted_iota(jnp.int32, (D_CHUNK, MIXER_WIDTH), 1) // HEAD_DIM
    for n in range(tm // D_CHUNK):
        rows = slice(n * D_CHUNK, (n + 1) * D_CHUNK)
        vchunk = vv[rows, :]
        sv = bs_ref[...]
        for g in range(D_GROUPS):
            mixed = jnp.dot(ws_ref[g], vchunk, preferred_element_type=F32)
            sv = sv + jnp.where(lane_group == g, mixed, 0.0)
        yd_o[rows, :] = (u[rows, :] * sv).astype(BF16)


def _proj_call(h, gmix, win, gq, gk, cos_t, sin_t, lng, lnb, ws, bs, seq):
    rows, d = h.shape
    tm = min(ROW_TILE, seq)
    steps_per_seq = seq // tm
    row_spec = lambda w: pl.BlockSpec((tm, w), lambda i: (i, 0))
    full = lambda a: pl.BlockSpec(a.shape, lambda i: (0,) * a.ndim)
    tab_spec = pl.BlockSpec((tm, LANES), lambda i: (i % steps_per_seq, 0))
    row_out = (row_spec(MIXER_WIDTH), jax.ShapeDtypeStruct((rows, MIXER_WIDTH), BF16))
    def qt_out(tile):
        tq = min(tile, seq)
        return (pl.BlockSpec((tm // tq, MIXER_WIDTH, tq), lambda i: (i, 0, 0)),
                jax.ShapeDtypeStruct((rows // tq, MIXER_WIDTH, tq), BF16))
    def vt_out(heads):
        r = heads * VT_ROWS
        return (pl.BlockSpec((None, r, tm), lambda i: (i, 0, 0)),
                jax.ShapeDtypeStruct((rows // tm, r, tm), BF16))
    outs = ([qt_out(A_Q_TILE), row_out, vt_out(A_KV_HEADS),
             qt_out(B_Q_TILE), row_out, vt_out(B_HEADS)] + [row_out] * 4)
    return pl.pallas_call(
        _proj_kernel,
        grid=(rows // tm,),
        in_specs=[row_spec(d), full(gmix), full(win), full(gq), full(gk), tab_spec, tab_spec,
                  full(lng), full(lnb), full(ws), full(bs)],
        out_specs=[o[0] for o in outs],
        out_shape=[o[1] for o in outs],
        compiler_params=pltpu.CompilerParams(
            dimension_semantics=("parallel",), vmem_limit_bytes=VMEM_LIMIT_BYTES),
        name="proj_mixers",
    )(h, gmix, win, gq, gk, cos_t, sin_t, lng, lnb, ws, bs)


def _online_softmax_step(s, m_cur, vts, m_scr, acc_scr, col_offset=None):
    m_prev = m_scr[...]
    if col_offset is not None:
        m_cur = m_cur - col_offset
    m_next = jnp.maximum(m_prev, m_cur)
    alpha = jnp.exp2(m_prev - m_next)
    shift = m_next if col_offset is None else m_next + col_offset
    p = jnp.exp2(s - shift).astype(BF16)
    width = p.shape[1] // len(vts)
    pv = [jnp.dot(vt, p[:, j * width:(j + 1) * width], preferred_element_type=F32)
          for j, vt in enumerate(vts)]
    acc_scr[...] = alpha * acc_scr[...] + jnp.concatenate(pv, axis=1)
    m_scr[...] = m_next


def _stack_masked_t(qt, groups):
    qt = qt.astype(F32)
    width = qt.shape[0] // groups
    row = lax.broadcasted_iota(jnp.int32, qt.shape, 0) // width
    zero = jnp.zeros_like(qt)
    return jnp.concatenate([jnp.where(row == j, qt, zero) for j in range(groups)],
                           axis=1).astype(BF16)


def _stack_masked(q, groups):
    width = q.shape[1] // groups
    lane = lax.broadcasted_iota(jnp.int32, q.shape, 1) // width
    zero = jnp.zeros_like(q)
    return jnp.concatenate([jnp.where(lane == j, q, zero) for j in range(groups)], axis=0)


def _init_softmax_state(m_scr, acc_scr):
    m_scr[...] = jnp.full(m_scr.shape, NEG_BIG, F32)
    acc_scr[...] = jnp.zeros(acc_scr.shape, F32)


def _normalized(acc_scr):
    return acc_scr[:HEAD_DIM, :] / acc_scr[HEAD_DIM:HEAD_DIM + 1, :]


def _dense_specs(b, seq, tq, nk, vt_rows, tk):
    qt_spec = pl.BlockSpec((None, seq // tq, LANES, tq), lambda bi, g, qi: (bi, 0, g, 0))
    k_spec = pl.BlockSpec((None, seq, LANES), lambda bi, g, qi: (bi, 0, g))
    vt_spec = pl.BlockSpec((None, nk, vt_rows, tk), lambda bi, g, qi: (bi, 0, g, 0))
    o_spec = pl.BlockSpec((None, tq, LANES), lambda bi, g, qi: (bi, qi, g))
    return qt_spec, k_spec, vt_spec, o_spec


def _pipelined_chunks(nk, lookahead, first_tile, scores, consume, s_scr, max_scr):
    unroll = PIPELINE_UNROLL if nk % PIPELINE_UNROLL == 0 else s_scr.shape[0]
    slots = s_scr.shape[0]
    assert nk % unroll == 0 and unroll % slots == 0 and lookahead < slots <= nk

    def produce(ahead, c, slot):
        s = scores(ahead, c)
        s_scr[slot] = s
        max_scr[slot] = jnp.max(s, axis=0, keepdims=True)

    @pl.when(first_tile)
    def _():
        for c in range(lookahead):
            produce(0, c, c)

    def body(i, carry):
        for j in range(unroll):
            c = unroll * i + j
            ahead = c + lookahead
            if j + lookahead < unroll:
                produce(0, ahead, (j + lookahead) % slots)
            else:
                wrap = (ahead >= nk).astype(jnp.int32)
                produce(wrap, ahead - wrap * nk, (j + lookahead) % slots)
            consume(c, s_scr[j % slots], max_scr[j % slots])
        return carry

    lax.fori_loop(0, nk // unroll, body, 0)


def _stage_query_tiles(qt_ref, qs_scr, qi, groups):
    last = qt_ref.shape[0] - 1

    @pl.when(qi == 0)
    def _():
        qs_scr[0] = _stack_masked_t(qt_ref[0], groups)

    nxt = jnp.minimum(qi + 1, last)
    qs_scr[(qi + 1) % 2] = _stack_masked_t(qt_ref[nxt], groups)
    return nxt


def _gqa_kernel(qt_ref, k_ref, vt_ref, o_ref, qs_scr, s_scr, max_scr, m_scr, acc_scr):
    tq = qt_ref.shape[2]
    nk, _, tk = vt_ref.shape
    qi = pl.program_id(2)
    _stage_query_tiles(qt_ref, qs_scr, qi, 2)
    _init_softmax_state(m_scr, acc_scr)

    def scores(ahead, c):
        start = pl.multiple_of(c * tk, tk)
        return jnp.dot(k_ref[pl.ds(start, tk), :], qs_scr[(qi + ahead) % 2],
                       preferred_element_type=F32)

    def consume(c, s, m_cur):
        _online_softmax_step(s, m_cur, [vt_ref[c]], m_scr, acc_scr)

    _pipelined_chunks(nk, A_LOOKAHEAD, qi == 0, scores, consume, s_scr, max_scr)
    o = _normalized(acc_scr)
    o_ref[...] = jnp.concatenate([o[:, :tq], o[:, tq:]], axis=0).T.astype(o_ref.dtype)


def _gqa_call(qt, k, vt):
    b, nk, vt_rows, tk = vt.shape
    vt_rows //= A_KV_HEADS
    seq = k.shape[1]
    tq = min(A_Q_TILE, seq)
    qt_spec, k_spec, vt_spec, o_spec = _dense_specs(b, seq, tq, nk, vt_rows, tk)
    return pl.pallas_call(
        _gqa_kernel,
        grid=(b, A_KV_HEADS, seq // tq),
        in_specs=[qt_spec, k_spec, vt_spec],
        out_specs=o_spec,
        out_shape=jax.ShapeDtypeStruct((b, seq, MIXER_WIDTH), BF16),
        scratch_shapes=[pltpu.VMEM((2, LANES, 2 * tq), BF16),
                        pltpu.VMEM((SCORE_SLOTS, tk, 2 * tq), F32),
                        pltpu.VMEM((SCORE_SLOTS, 1, 2 * tq), F32),
                        pltpu.VMEM((1, 2 * tq), F32),
                        pltpu.VMEM((VT_ROWS, 2 * tq), F32)],
        compiler_params=pltpu.CompilerParams(
            dimension_semantics=("parallel", "parallel", "arbitrary"),
            vmem_limit_bytes=VMEM_LIMIT_BYTES),
        name="gqa_attention",
    )(qt, k, vt)


def _diff_kernel(scal_ref, qt_ref, k_ref, vt_ref, lq_ref, lk_ref, gsub_ref, o_ref,
                 qs_scr, s_scr, max_scr, key_scr, m_scr, acc_scr):
    tq = qt_ref.shape[2]
    nk, _, tk = vt_ref.shape
    pair = pl.program_id(1)
    qi = pl.program_id(2)
    lam_init = scal_ref[0]
    slope_a = scal_ref[1 + 2 * pair]
    slope_b = scal_ref[2 + 2 * pair]
    nxt = _stage_query_tiles(qt_ref, qs_scr, qi, 4)
    _init_softmax_state(m_scr, acc_scr)

    q0 = qi * tq
    c_mid = q0 // tk
    col = lax.broadcasted_iota(jnp.int32, (1, 4 * tq), 1)
    col_slope = jnp.where(col < 2 * tq, slope_a, slope_b)
    col_pos = (q0 + col % tq).astype(F32)
    key_off = lax.broadcasted_iota(jnp.int32, (tk, 4 * tq), 0)

    tiles_per_chunk = tk // tq

    @pl.when(qi == 0)
    def _():
        key_table = key_off.astype(F32) * col_slope
        key_scr[0] = key_table
        key_scr[1] = -key_table
        rel = lax.broadcasted_iota(jnp.int32, (tk, 4 * tq), 1) % tq - key_off
        for t in range(tiles_per_chunk):
            key_scr[2 + t] = -(jnp.abs(rel + t * tq).astype(F32) * col_slope)

    def scores(ahead, c):
        tile = jnp.where(ahead == 0, qi, nxt)
        mid = tile // tiles_per_chunk
        table = jnp.where(c == mid, 2 + tile % tiles_per_chunk, jnp.where(c < mid, 0, 1))
        start = pl.multiple_of(c * tk, tk)
        s = jnp.dot(k_ref[pl.ds(start, tk), :], qs_scr[(qi + ahead) % 2],
                    preferred_element_type=F32)
        return s + key_scr[table]

    def consume(c, s, m_cur):
        sign = jnp.where(c == c_mid, 0.0, jnp.where(c < c_mid, 1.0, -1.0)).astype(F32)
        col_offset = col_slope * (sign * (col_pos - (c * tk).astype(F32)))
        vts = [vt_ref[c, :VT_ROWS, :], vt_ref[c, VT_ROWS:, :]]
        _online_softmax_step(s, m_cur, vts, m_scr, acc_scr, col_offset)

    _pipelined_chunks(nk, B_LOOKAHEAD, qi == 0, scores, consume, s_scr, max_scr)

    t = jnp.exp(jnp.sum(lq_ref[...] * lk_ref[...], axis=1, keepdims=True))
    lam = t[0:1, :] - t[1:2, :] + lam_init
    o = _normalized(acc_scr)
    o_a = o[:, 0:tq] - lam * o[:, tq:2 * tq]
    o_b = o[:, 2 * tq:3 * tq] - lam * o[:, 3 * tq:]
    def head_rms(y):
        return y * lax.rsqrt(jnp.mean(y * y, axis=0, keepdims=True) + RMS_EPS)

    o = jnp.concatenate([head_rms(o_a), head_rms(o_b)], axis=0).T
    o_ref[...] = (o * gsub_ref[...] * (1.0 - lam_init)).astype(o_ref.dtype)


def _diff_call(scal, qt, k, vt, lq, lk, gsub):
    b, nk, vt_rows, tk = vt.shape
    vt_rows //= B_HEADS // 2
    seq = k.shape[1]
    tq = min(B_Q_TILE, seq)
    assert tk % tq == 0
    qt_spec, k_spec, vt_spec, o_spec = _dense_specs(b, seq, tq, nk, vt_rows, tk)
    full = lambda a: pl.BlockSpec(a.shape, lambda bi, g, qi: (0,) * a.ndim)
    return pl.pallas_call(
        _diff_kernel,
        grid=(b, B_HEADS // 2, seq // tq),
        in_specs=[pl.BlockSpec(memory_space=pltpu.SMEM), qt_spec, k_spec, vt_spec,
                  full(lq), full(lk), full(gsub)],
        out_specs=o_spec,
        out_shape=jax.ShapeDtypeStruct((b, seq, MIXER_WIDTH), BF16),
        scratch_shapes=[pltpu.VMEM((2, LANES, 4 * tq), BF16),
                        pltpu.VMEM((SCORE_SLOTS, tk, 4 * tq), F32),
                        pltpu.VMEM((SCORE_SLOTS, 1, 4 * tq), F32),
                        pltpu.VMEM((2 + tk // tq, tk, 4 * tq), F32),
                        pltpu.VMEM((1, 4 * tq), F32),
                        pltpu.VMEM((VT_ROWS, 4 * tq), F32)],
        compiler_params=pltpu.CompilerParams(
            dimension_semantics=("parallel", "parallel", "arbitrary"),
            vmem_limit_bytes=VMEM_LIMIT_BYTES),
        name="diff_attention",
    )(scal, qt, k, vt, lq, lk, gsub)


def _nbr_kernel(q_ref, kp_ref, kc_ref, kn_ref, vp_ref, vc_ref, vn_ref, bias_ref, o_ref,
                kband, vband, *, grid_rows):
    blk = q_ref.shape[0]
    rb = blk // GRID_W
    i = pl.program_id(1)
    kband[0:blk] = kp_ref[...]
    kband[blk:2 * blk] = kc_ref[...]
    kband[2 * blk:] = kn_ref[...]
    vband[0:blk] = vp_ref[...]
    vband[blk:2 * blk] = vc_ref[...]
    vband[2 * blk:] = vn_ref[...]
    band = NA_ROWS * GRID_W
    lane_head = lax.broadcasted_iota(jnp.int32, (GRID_W, MIXER_WIDTH), 1) // HEAD_DIM
    for jr in range(rb):
        r = i * rb + jr
        rs = jnp.clip(r - NA_ROWS // 2, 0, grid_rows - NA_ROWS)
        start = pl.multiple_of((rs - (i - 1) * rb) * GRID_W, GRID_W)
        k = kband[pl.ds(start, band), :]
        v = vband[pl.ds(start, band), :]
        qs = _stack_masked(q_ref[jr * GRID_W:(jr + 1) * GRID_W, :], C_HEADS)
        s = lax.dot_general(qs, k, (((1,), (1,)), ((), ())), preferred_element_type=F32)
        s = s + bias_ref[r - rs].reshape(C_HEADS * GRID_W, band)
        m = jnp.max(s, axis=1, keepdims=True)
        p = jnp.exp2(s - m)
        l = jnp.sum(p, axis=1, keepdims=True)
        pv = jnp.dot(p.astype(BF16), v, preferred_element_type=F32) / l
        o = jnp.zeros((GRID_W, MIXER_WIDTH), F32)
        for hd in range(C_HEADS):
            o = jnp.where(lane_head == hd, pv[hd * GRID_W:(hd + 1) * GRID_W], o)
        o_ref[jr * GRID_W:(jr + 1) * GRID_W, :] = o.astype(o_ref.dtype)


def _nbr_call(q, k, v, bias):
    b, seq, w = q.shape
    grid_rows = seq // GRID_W
    rb = min(C_BLOCK_ROWS, grid_rows)
    blk = rb * GRID_W
    nblk = seq // blk
    cur = pl.BlockSpec((None, blk, w), lambda bi, i: (bi, i, 0))
    prev = pl.BlockSpec((None, blk, w), lambda bi, i: (bi, jnp.maximum(i - 1, 0), 0))
    nxt = pl.BlockSpec((None, blk, w), lambda bi, i: (bi, jnp.minimum(i + 1, nblk - 1), 0))
    return pl.pallas_call(
        functools.partial(_nbr_kernel, grid_rows=grid_rows),
        grid=(b, nblk),
        in_specs=[cur, prev, cur, nxt, prev, cur, nxt,
                  pl.BlockSpec(bias.shape, lambda bi, i: (0, 0, 0, 0))],
        out_specs=cur,
        out_shape=jax.ShapeDtypeStruct((b, seq, w), BF16),
        scratch_shapes=[pltpu.VMEM((3 * blk, w), BF16), pltpu.VMEM((3 * blk, w), BF16)],
        compiler_params=pltpu.CompilerParams(
            dimension_semantics=("parallel", "parallel"), vmem_limit_bytes=VMEM_LIMIT_BYTES),
        name="nbr_attention",
    )(q, k, k, k, v, v, v, bias)


def _nbr_bias_table(rpb):
    qc = np.arange(GRID_W)
    kc = np.arange(GRID_W)
    cs = np.clip(qc - NA_COLS // 2, 0, GRID_W - NA_COLS)
    valid = (kc[None, :] >= cs[:, None]) & (kc[None, :] < cs[:, None] + NA_COLS)
    dc = kc[None, :] - qc[:, None] + (NA_COLS - 1)
    d = np.arange(NA_ROWS)
    a = np.arange(NA_ROWS)
    dr = a[None, :] - d[:, None] + (NA_ROWS - 1)
    pick_r = (dr[:, :, None] == np.arange(2 * NA_ROWS - 1)).astype(np.float32)
    pick_c = ((dc[:, :, None] == np.arange(2 * NA_COLS - 1)) & valid[:, :, None]).astype(np.float32)
    tbl = jnp.einsum("hrc,dar,qkc->dhqak", rpb.astype(F32) * LOG2E, pick_r, pick_c,
                     precision=lax.Precision.HIGHEST)
    tbl = jnp.where(valid[None, None, :, None, :], tbl, NEG_BIG)
    return tbl.reshape(NA_ROWS, C_HEADS, GRID_W, NA_ROWS * GRID_W)


def _channel_kernel(h_ref, ya_ref, yb_ref, yc_ref, yd_ref, wout_ref, gffn_ref, wg_ref, wu_ref,
                    wd_ref, gple_ref, wpg_ref, p_ref, wpp_ref, gfin_ref, o_ref, *, final):
    w = MIXER_WIDTH
    h = h_ref[...]
    for j, y_ref in enumerate((ya_ref, yb_ref, yc_ref, yd_ref)):
        h = h + jnp.dot(y_ref[...], wout_ref[j * w:(j + 1) * w, :], preferred_element_type=F32)

    hn = _rms(h, gffn_ref[...]).astype(BF16)
    ffn = wg_ref.shape[1]
    acc = jnp.zeros(h.shape, F32)
    for c in range(ffn // FFN_CHUNK):
        cols = slice(c * FFN_CHUNK, (c + 1) * FFN_CHUNK)
        gate = jnp.dot(hn, wg_ref[:, cols], preferred_element_type=F32)
        up = jnp.dot(hn, wu_ref[:, cols], preferred_element_type=F32)
        act = (gate * jax.nn.sigmoid(gate) * up).astype(BF16)
        acc = acc + jnp.dot(act, wd_ref[cols, :], preferred_element_type=F32)
    h = h + acc

    hn = _rms(h, gple_ref[...]).astype(BF16)
    gate = jax.nn.sigmoid(jnp.dot(hn, wpg_ref[...], preferred_element_type=F32))
    emb = jnp.dot(p_ref[...].astype(BF16), wpp_ref[...], preferred_element_type=F32)
    h = h + gate * emb
    if final:
        h = _rms(h, gfin_ref[...])
    o_ref[...] = h


def _channel_call(h, ya, yb, yc, yd, wout, gffn, wg, wu, wd, gple, wpg, p, wpp, gfin, final, seq):
    rows, d = h.shape
    tm = min(ROW_TILE, seq)
    row_spec = lambda w: pl.BlockSpec((tm, w), lambda i: (i, 0))
    resident = lambda a: pl.BlockSpec(a.shape, lambda i: (0,) * a.ndim,
                                      pipeline_mode=pl.Buffered(1))
    return pl.pallas_call(
        functools.partial(_channel_kernel, final=final),
        grid=(rows // tm,),
        in_specs=[row_spec(d)] + [row_spec(MIXER_WIDTH)] * 4
                 + [resident(wout), resident(gffn), resident(wg), resident(wu), resident(wd),
                    resident(gple), resident(wpg), row_spec(p.shape[1]), resident(wpp),
                    resident(gfin)],
        out_specs=row_spec(d),
        out_shape=jax.ShapeDtypeStruct((rows, d), F32),
        compiler_params=pltpu.CompilerParams(
            dimension_semantics=("parallel",), vmem_limit_bytes=VMEM_LIMIT_BYTES),
        name="channel_mixers",
    )(h, ya, yb, yc, yd, wout, gffn, wg, wu, wd, gple, wpg, p, wpp, gfin)


def _rope_tables(seq):
    t = jnp.arange(seq)
    row = (t // GRID_W).astype(F32)
    col = (t % GRID_W).astype(F32)
    n_freq = HEAD_DIM // 4
    inv = ROPE_THETA ** (-jnp.arange(n_freq, dtype=F32) / n_freq)
    ang = jnp.concatenate([row[:, None] * inv, col[:, None] * inv], axis=-1)
    cos, sin = jnp.cos(ang), jnp.sin(ang)
    heads = LANES // HEAD_DIM
    return (jnp.tile(jnp.concatenate([cos, cos], axis=-1), (1, heads)),
            jnp.tile(jnp.concatenate([-sin, sin], axis=-1), (1, heads)))


def kernel(x, p, g_mix, w_in, a_q_norm, a_k_norm, b_lam_q, b_lam_k, b_sub_norm, c_rpb, d_ln_g,
           d_ln_b, d_w_s, d_b_s, w_out, g_ffn, w_gate, w_up, w_down, g_ple, w_ple_gate,
           w_ple_proj, g_final):
    b, seq, d = x.shape
    depth = w_in.shape[0]
    assert seq % GRID_W == 0 and seq // GRID_W >= NA_ROWS and seq % D_CHUNK == 0
    rows = b * seq
    h = x.reshape(rows, d)
    cos_t, sin_t = _rope_tables(seq)
    slopes = (2.0 ** (-8.0 / B_HEADS)) ** jnp.arange(1, B_HEADS + 1, dtype=F32) * LOG2E
    row2 = lambda a: a.reshape(1, -1).astype(F32)
    for i in range(depth):
        bs_tile = jnp.repeat(d_b_s[i].T, HEAD_DIM, axis=1)
        aqt, ak, avt, bqt, bk, bvt, cq, ck, cv, yd = _proj_call(
            h, row2(g_mix[i]), w_in[i].astype(BF16),
            row2(jnp.tile(a_q_norm[i], A_HEADS)), row2(jnp.tile(a_k_norm[i], A_KV_HEADS)),
            cos_t, sin_t, row2(d_ln_g[i]), row2(d_ln_b[i]), d_w_s[i].astype(BF16),
            bs_tile.astype(F32), seq)
        sh = lambda a: a.reshape(b, seq, MIXER_WIDTH)
        vt4 = lambda a: a.reshape(b, -1, a.shape[-2], a.shape[-1])
        ya = _gqa_call(vt4(aqt), sh(ak), vt4(avt))
        lam_init = 0.8 - 0.6 * math.exp(-0.3 * i)
        scal = jnp.concatenate([jnp.full((1,), lam_init, F32), slopes])
        yb = _diff_call(scal, vt4(bqt), sh(bk), vt4(bvt), b_lam_q[i].astype(F32),
                        b_lam_k[i].astype(F32), row2(jnp.tile(b_sub_norm[i], LANES // HEAD_DIM)))
        yc = _nbr_call(sh(cq), sh(ck), sh(cv), _nbr_bias_table(c_rpb[i]))
        fl = lambda a: a.reshape(rows, MIXER_WIDTH)
        h = _channel_call(
            h, fl(ya), fl(yb), fl(yc), yd, w_out[i].astype(BF16), row2(g_ffn[i]),
            w_gate[i].astype(BF16), w_up[i].astype(BF16), w_down[i].astype(BF16),
            row2(g_ple[i]), w_ple_gate[i].astype(BF16), p[i].reshape(rows, -1),
            w_ple_proj[i].astype(BF16), row2(g_final), i == depth - 1, seq)
    return h.reshape(b, seq, d)
```

```python
import functools
import math

import jax
import jax.numpy as jnp
import numpy as np
from jax import lax
from jax.experimental import pallas as pl
from jax.experimental.pallas import tpu as pltpu

F32 = jnp.float32
BF16 = jnp.bfloat16

HEAD_DIM = 64
GRID_W = 64
RMS_EPS = 1e-6
LN_EPS = 1e-5
A_HEADS = 4
A_KV_HEADS = 2
ROPE_THETA = 10000.0
B_HEADS = 4
B_QK_DIM = HEAD_DIM // 2
C_HEADS = 4
NA_ROWS = 8
NA_COLS = 16
D_GROUPS = 4
D_CHUNK = 128
MIXER_WIDTH = 256

LOG2E = math.log2(math.e)
NEG_BIG = -1e30

LANES = 128
BF16_SUBLANES = 16
VMEM_LIMIT_BYTES = 56 * 1024 * 1024

VT_ROWS = HEAD_DIM + BF16_SUBLANES

ROW_TILE = 512
A_Q_TILE = 256
B_Q_TILE = 128
C_BLOCK_ROWS = 8
FFN_CHUNK = 256
PIPELINE_UNROLL = 32
TILES_PER_STEP = 2
SCORE_SLOTS = 4
A_LOOKAHEAD = 2
B_LOOKAHEAD = 1


def _group_ones(width, group):
    r = lax.broadcasted_iota(jnp.int32, (width, width), 0) // group
    c = lax.broadcasted_iota(jnp.int32, (width, width), 1) // group
    return (r == c).astype(F32)


def _group_sum(x, group):
    return jnp.dot(x, _group_ones(x.shape[-1], group), precision=lax.Precision.HIGHEST,
                   preferred_element_type=F32)


def _rms(x, g):
    ms = jnp.mean(x * x, axis=-1, keepdims=True)
    return x * lax.rsqrt(ms + RMS_EPS) * g


def _proj_kernel(h_ref, gmix_ref, win_ref, gq_ref, gk_ref, cos_ref, sin_ref,
                 lng_ref, lnb_ref, ws_ref, bs_ref,
                 aqt_o, ak_o, avt_o, bqt_o, bk_o, bvt_o, cq_o, ck_o, cv_o, yd_o):
    tm = h_ref.shape[0]
    hn = _rms(h_ref[...], gmix_ref[...]).astype(BF16)

    def proj(lo, width):
        return jnp.dot(hn, win_ref[:, lo:lo + width], preferred_element_type=F32)

    cos = cos_ref[...]
    sin = sin_ref[...]

    def qk_norm_rope(y, gain, cos_t, sin_t):
        w = y.shape[-1]
        ss = _group_sum(y * y, HEAD_DIM)
        y = y * lax.rsqrt(ss * (1.0 / HEAD_DIM) + RMS_EPS) * gain
        lane = lax.broadcasted_iota(jnp.int32, y.shape, 1)
        half = HEAD_DIM // 2
        swapped = jnp.where(lane % HEAD_DIM < half,
                            pltpu.roll(y, w - half, axis=1), pltpu.roll(y, half, axis=1))
        return y * cos_t + swapped * sin_t

    a_scale = HEAD_DIM ** -0.5 * LOG2E
    cos2 = jnp.concatenate([cos, cos], axis=1)
    sin2 = jnp.concatenate([sin, sin], axis=1)
    aq = qk_norm_rope(proj(0, 256), gq_ref[...], cos2, sin2) * a_scale
    def store_tiles(o_ref, xt):
        tq = o_ref.shape[-1]
        for j in range(o_ref.shape[0]):
            o_ref[j] = xt[:, j * tq:(j + 1) * tq].astype(BF16)

    store_tiles(aqt_o, aq.T)
    ak = qk_norm_rope(proj(256, 128), gk_ref[...], cos, sin)
    av = proj(384, 128)
    lane128 = lax.broadcasted_iota(jnp.int32, (tm, LANES), 1)
    first = lane128 < HEAD_DIM

    def dup_heads(y):
        r = pltpu.roll(y, HEAD_DIM, axis=1)
        return jnp.concatenate([jnp.where(first, y, r), jnp.where(first, r, y)], axis=1)

    ak_o[...] = dup_heads(ak).astype(BF16)
    ones = jnp.ones((BF16_SUBLANES, tm), F32)
    avt = av.T
    avt_o[...] = jnp.concatenate(
        [avt[:HEAD_DIM], ones, avt[HEAD_DIM:], ones], axis=0).astype(BF16)

    store_tiles(bqt_o, (proj(512, 256) * (B_QK_DIM ** -0.5 * LOG2E)).T)
    bk_o[...] = proj(768, 256).astype(BF16)
    bvt = proj(1024, 256).T
    bvt_o[...] = jnp.concatenate(
        [x for hd in range(B_HEADS) for x in (bvt[hd * HEAD_DIM:(hd + 1) * HEAD_DIM], ones)],
        axis=0).astype(BF16)

    cq_o[...] = (proj(1280, 256) * a_scale).astype(BF16)
    ck_o[...] = proj(1536, 256).astype(BF16)
    cv_o[...] = proj(1792, 256).astype(BF16)

    def gelu(t):
        return 0.5 * t * (1.0 + lax.erf(t * (2.0 ** -0.5)))

    u = gelu(proj(2048, 256))
    vv = gelu(proj(2304, 256))
    mu = jnp.mean(vv, axis=-1, keepdims=True)
    vc = vv - mu
    var = jnp.mean(vc * vc, axis=-1, keepdims=True)
    vv = (vc * lax.rsqrt(var + LN_EPS) * lng_ref[...] + lnb_ref[...]).astype(BF16)
    lane_group = lax.broadcasted_iota(jnp.int32, (D_CHUNK, MIXER_WIDTH), 1) // HEAD_DIM
    for n in range(tm // D_CHUNK):
        rows = slice(n * D_CHUNK, (n + 1) * D_CHUNK)
        vchunk = vv[rows, :]
        sv = bs_ref[...]
        for g in range(D_GROUPS):
            mixed = jnp.dot(ws_ref[g], vchunk, preferred_element_type=F32)
            sv = sv + jnp.where(lane_group == g, mixed, 0.0)
        yd_o[rows, :] = (u[rows, :] * sv).astype(BF16)


def _proj_call(h, gmix, win, gq, gk, cos_t, sin_t, lng, lnb, ws, bs, seq):
    rows, d = h.shape
    tm = min(ROW_TILE, seq)
    steps_per_seq = seq // tm
    row_spec = lambda w: pl.BlockSpec((tm, w), lambda i: (i, 0))
    full = lambda a: pl.BlockSpec(a.shape, lambda i: (0,) * a.ndim)
    tab_spec = pl.BlockSpec((tm, LANES), lambda i: (i % steps_per_seq, 0))
    row_out = (row_spec(MIXER_WIDTH), jax.ShapeDtypeStruct((rows, MIXER_WIDTH), BF16))
    def qt_out(tile):
        tq = min(tile, seq)
        return (pl.BlockSpec((tm // tq, MIXER_WIDTH, tq), lambda i: (i, 0, 0)),
                jax.ShapeDtypeStruct((rows // tq, MIXER_WIDTH, tq), BF16))
    def vt_out(heads):
        r = heads * VT_ROWS
        return (pl.BlockSpec((None, r, tm), lambda i: (i, 0, 0)),
                jax.ShapeDtypeStruct((rows // tm, r, tm), BF16))
    outs = ([qt_out(A_Q_TILE), row_out, vt_out(A_KV_HEADS),
             qt_out(B_Q_TILE), row_out, vt_out(B_HEADS)] + [row_out] * 4)
    return pl.pallas_call(
        _proj_kernel,
        grid=(rows // tm,),
        in_specs=[row_spec(d), full(gmix), full(win), full(gq), full(gk), tab_spec, tab_spec,
                  full(lng), full(lnb), full(ws), full(bs)],
        out_specs=[o[0] for o in outs],
        out_shape=[o[1] for o in outs],
        compiler_params=pltpu.CompilerParams(
            dimension_semantics=("parallel",), vmem_limit_bytes=VMEM_LIMIT_BYTES),
        name="proj_mixers",
    )(h, gmix, win, gq, gk, cos_t, sin_t, lng, lnb, ws, bs)


def _online_softmax_step(s, m_cur, vts, m_scr, acc_scr, col_offset=None):
    m_prev = m_scr[...]
    if col_offset is not None:
        m_cur = m_cur - col_offset
    m_next = jnp.maximum(m_prev, m_cur)
    alpha = jnp.exp2(m_prev - m_next)
    shift = m_next if col_offset is None else m_next + col_offset
    p = jnp.exp2(s - shift).astype(BF16)
    width = p.shape[1] // len(vts)
    pv = [jnp.dot(vt, p[:, j * width:(j + 1) * width], preferred_element_type=F32)
          for j, vt in enumerate(vts)]
    acc_scr[...] = alpha * acc_scr[...] + jnp.concatenate(pv, axis=1)
    m_scr[...] = m_next


def _stack_masked_t(qt, groups):
    qt = qt.astype(F32)
    width = qt.shape[0] // groups
    row = lax.broadcasted_iota(jnp.int32, qt.shape, 0) // width
    zero = jnp.zeros_like(qt)
    return jnp.concatenate([jnp.where(row == j, qt, zero) for j in range(groups)],
                           axis=1).astype(BF16)


def _stack_masked(q, groups):
    width = q.shape[1] // groups
    lane = lax.broadcasted_iota(jnp.int32, q.shape, 1) // width
    zero = jnp.zeros_like(q)
    return jnp.concatenate([jnp.where(lane == j, q, zero) for j in range(groups)], axis=0)


def _init_softmax_state(m_scr, acc_scr):
    m_scr[...] = jnp.full(m_scr.shape, NEG_BIG, F32)
    acc_scr[...] = jnp.zeros(acc_scr.shape, F32)


def _normalized(acc_scr):
    return acc_scr[:HEAD_DIM, :] / acc_scr[HEAD_DIM:HEAD_DIM + 1, :]


def _dense_specs(b, seq, tq, nk, vt_rows, tk):
    qt_spec = pl.BlockSpec((None, seq // tq, LANES, tq), lambda bi, g, qi: (bi, 0, g, 0))
    k_spec = pl.BlockSpec((None, seq, LANES), lambda bi, g, qi: (bi, 0, g))
    vt_spec = pl.BlockSpec((None, nk, vt_rows, tk), lambda bi, g, qi: (bi, 0, g, 0))
    o_spec = pl.BlockSpec((None, tq * TILES_PER_STEP, LANES), lambda bi, g, qi: (bi, qi, g))
    return qt_spec, k_spec, vt_spec, o_spec


def _pipelined_chunks(nk, lookahead, first_tile, scores, consume, s_scr, max_scr):
    unroll = PIPELINE_UNROLL if nk % PIPELINE_UNROLL == 0 else s_scr.shape[0]
    slots = s_scr.shape[0]
    assert nk % unroll == 0 and unroll % slots == 0 and lookahead < slots <= nk

    def produce(ahead, c, slot):
        s = scores(ahead, c)
        s_scr[slot] = s
        max_scr[slot] = jnp.max(s, axis=0, keepdims=True)

    @pl.when(first_tile)
    def _():
        for c in range(lookahead):
            produce(0, c, c)

    def body(i, carry):
        for j in range(unroll):
            c = unroll * i + j
            ahead = c + lookahead
            if j + lookahead < unroll:
                produce(0, ahead, (j + lookahead) % slots)
            else:
                wrap = int(ahead >= nk) if isinstance(ahead, int) else (ahead >= nk).astype(jnp.int32)
                produce(wrap, ahead - wrap * nk, (j + lookahead) % slots)
            consume(c, s_scr[j % slots], max_scr[j % slots])
        return carry

    if nk == unroll:
        body(0, 0)
    else:
        lax.fori_loop(0, nk // unroll, body, 0)


def _stage_query_tiles(qt_ref, qs_scr, qi, groups):
    last = qt_ref.shape[0] - 1

    @pl.when(qi == 0)
    def _():
        qs_scr[0] = _stack_masked_t(qt_ref[0], groups)

    nxt = jnp.minimum(qi + 1, last)
    qs_scr[(qi + 1) % 2] = _stack_masked_t(qt_ref[nxt], groups)
    return nxt


def _gqa_kernel(qt_ref, k_ref, vt_ref, o_ref, qs_scr, s_scr, max_scr, m_scr, acc_scr):
    tq = qt_ref.shape[2]
    for t in range(TILES_PER_STEP):
        _gqa_tile(pl.program_id(2) * TILES_PER_STEP + t, qt_ref, k_ref, vt_ref,
                  o_ref.at[t * tq:(t + 1) * tq, :], qs_scr, s_scr, max_scr, m_scr, acc_scr)


def _gqa_tile(qi, qt_ref, k_ref, vt_ref, o_ref, qs_scr, s_scr, max_scr, m_scr, acc_scr):
    tq = qt_ref.shape[2]
    nk, _, tk = vt_ref.shape
    _stage_query_tiles(qt_ref, qs_scr, qi, 2)
    _init_softmax_state(m_scr, acc_scr)

    def scores(ahead, c):
        start = pl.multiple_of(c * tk, tk)
        return jnp.dot(k_ref[pl.ds(start, tk), :], qs_scr[(qi + ahead) % 2],
                       preferred_element_type=F32)

    def consume(c, s, m_cur):
        _online_softmax_step(s, m_cur, [vt_ref[c]], m_scr, acc_scr)

    _pipelined_chunks(nk, A_LOOKAHEAD, qi == 0, scores, consume, s_scr, max_scr)
    o = _normalized(acc_scr)
    o_ref[...] = jnp.concatenate([o[:, :tq], o[:, tq:]], axis=0).T.astype(o_ref.dtype)


def _gqa_call(qt, k, vt):
    b, nk, vt_rows, tk = vt.shape
    vt_rows //= A_KV_HEADS
    seq = k.shape[1]
    tq = min(A_Q_TILE, seq)
    qt_spec, k_spec, vt_spec, o_spec = _dense_specs(b, seq, tq, nk, vt_rows, tk)
    return pl.pallas_call(
        _gqa_kernel,
        grid=(b, A_KV_HEADS, seq // (tq * TILES_PER_STEP)),
        in_specs=[qt_spec, k_spec, vt_spec],
        out_specs=o_spec,
        out_shape=jax.ShapeDtypeStruct((b, seq, MIXER_WIDTH), BF16),
        scratch_shapes=[pltpu.VMEM((2, LANES, 2 * tq), BF16),
                        pltpu.VMEM((SCORE_SLOTS, tk, 2 * tq), F32),
                        pltpu.VMEM((SCORE_SLOTS, 1, 2 * tq), F32),
                        pltpu.VMEM((1, 2 * tq), F32),
                        pltpu.VMEM((VT_ROWS, 2 * tq), F32)],
        compiler_params=pltpu.CompilerParams(
            dimension_semantics=("parallel", "parallel", "arbitrary"),
            vmem_limit_bytes=VMEM_LIMIT_BYTES),
        name="gqa_attention",
    )(qt, k, vt)


def _diff_kernel(scal_ref, qt_ref, k_ref, vt_ref, lq_ref, lk_ref, gsub_ref, o_ref,
                 qs_scr, s_scr, max_scr, key_scr, m_scr, acc_scr):
    tq = qt_ref.shape[2]
    for t in range(TILES_PER_STEP):
        _diff_tile(pl.program_id(2) * TILES_PER_STEP + t, scal_ref, qt_ref, k_ref, vt_ref,
                   lq_ref, lk_ref, gsub_ref, o_ref.at[t * tq:(t + 1) * tq, :],
                   qs_scr, s_scr, max_scr, key_scr, m_scr, acc_scr)


def _diff_tile(qi, scal_ref, qt_ref, k_ref, vt_ref, lq_ref, lk_ref, gsub_ref, o_ref,
               qs_scr, s_scr, max_scr, key_scr, m_scr, acc_scr):
    tq = qt_ref.shape[2]
    nk, _, tk = vt_ref.shape
    pair = pl.program_id(1)
    lam_init = scal_ref[0]
    slope_a = scal_ref[1 + 2 * pair]
    slope_b = scal_ref[2 + 2 * pair]
    nxt = _stage_query_tiles(qt_ref, qs_scr, qi, 4)
    _init_softmax_state(m_scr, acc_scr)

    q0 = qi * tq
    c_mid = q0 // tk
    col = lax.broadcasted_iota(jnp.int32, (1, 4 * tq), 1)
    col_slope = jnp.where(col < 2 * tq, slope_a, slope_b)
    col_pos = (q0 + col % tq).astype(F32)
    key_off = lax.broadcasted_iota(jnp.int32, (tk, 4 * tq), 0)

    tiles_per_chunk = tk // tq

    @pl.when(qi == 0)
    def _():
        key_table = key_off.astype(F32) * col_slope
        key_scr[0] = key_table
        key_scr[1] = -key_table
        rel = lax.broadcasted_iota(jnp.int32, (tk, 4 * tq), 1) % tq - key_off
        for t in range(tiles_per_chunk):
            key_scr[2 + t] = -(jnp.abs(rel + t * tq).astype(F32) * col_slope)

    def scores(ahead, c):
        tile = jnp.where(ahead == 0, qi, nxt)
        mid = tile // tiles_per_chunk
        table = jnp.where(c == mid, 2 + tile % tiles_per_chunk, jnp.where(c < mid, 0, 1))
        start = pl.multiple_of(c * tk, tk)
        s = jnp.dot(k_ref[pl.ds(start, tk), :], qs_scr[(qi + ahead) % 2],
                    preferred_element_type=F32)
        return s + key_scr[table]

    def consume(c, s, m_cur):
        sign = jnp.where(c == c_mid, 0.0, jnp.where(c < c_mid, 1.0, -1.0)).astype(F32)
        col_offset = col_slope * (sign * (col_pos - jnp.asarray(c * tk).astype(F32)))
        vts = [vt_ref[c, :VT_ROWS, :], vt_ref[c, VT_ROWS:, :]]
        _online_softmax_step(s, m_cur, vts, m_scr, acc_scr, col_offset)

    _pipelined_chunks(nk, B_LOOKAHEAD, qi == 0, scores, consume, s_scr, max_scr)

    t = jnp.exp(jnp.sum(lq_ref[...] * lk_ref[...], axis=1, keepdims=True))
    lam = t[0:1, :] - t[1:2, :] + lam_init
    o = _normalized(acc_scr)
    o_a = o[:, 0:tq] - lam * o[:, tq:2 * tq]
    o_b = o[:, 2 * tq:3 * tq] - lam * o[:, 3 * tq:]
    def head_rms(y):
        return y * lax.rsqrt(jnp.mean(y * y, axis=0, keepdims=True) + RMS_EPS)

    o = jnp.concatenate([head_rms(o_a), head_rms(o_b)], axis=0).T
    o_ref[...] = (o * gsub_ref[...] * (1.0 - lam_init)).astype(o_ref.dtype)


def _diff_call(scal, qt, k, vt, lq, lk, gsub):
    b, nk, vt_rows, tk = vt.shape
    vt_rows //= B_HEADS // 2
    seq = k.shape[1]
    tq = min(B_Q_TILE, seq)
    assert tk % tq == 0
    qt_spec, k_spec, vt_spec, o_spec = _dense_specs(b, seq, tq, nk, vt_rows, tk)
    full = lambda a: pl.BlockSpec(a.shape, lambda bi, g, qi: (0,) * a.ndim)
    return pl.pallas_call(
        _diff_kernel,
        grid=(b, B_HEADS // 2, seq // (tq * TILES_PER_STEP)),
        in_specs=[pl.BlockSpec(memory_space=pltpu.SMEM), qt_spec, k_spec, vt_spec,
                  full(lq), full(lk), full(gsub)],
        out_specs=o_spec,
        out_shape=jax.ShapeDtypeStruct((b, seq, MIXER_WIDTH), BF16),
        scratch_shapes=[pltpu.VMEM((2, LANES, 4 * tq), BF16),
                        pltpu.VMEM((SCORE_SLOTS, tk, 4 * tq), F32),
                        pltpu.VMEM((SCORE_SLOTS, 1, 4 * tq), F32),
                        pltpu.VMEM((2 + tk // tq, tk, 4 * tq), F32),
                        pltpu.VMEM((1, 4 * tq), F32),
                        pltpu.VMEM((VT_ROWS, 4 * tq), F32)],
        compiler_params=pltpu.CompilerParams(
            dimension_semantics=("parallel", "parallel", "arbitrary"),
            vmem_limit_bytes=VMEM_LIMIT_BYTES),
        name="diff_attention",
    )(scal, qt, k, vt, lq, lk, gsub)


def _nbr_kernel(q_ref, kp_ref, kc_ref, kn_ref, vp_ref, vc_ref, vn_ref, bias_ref, o_ref,
                kband, vband, *, grid_rows):
    blk = q_ref.shape[0]
    rb = blk // GRID_W
    i = pl.program_id(1)
    kband[0:blk] = kp_ref[...]
    kband[blk:2 * blk] = kc_ref[...]
    kband[2 * blk:] = kn_ref[...]
    vband[0:blk] = vp_ref[...]
    vband[blk:2 * blk] = vc_ref[...]
    vband[2 * blk:] = vn_ref[...]
    band = NA_ROWS * GRID_W
    lane_head = lax.broadcasted_iota(jnp.int32, (GRID_W, MIXER_WIDTH), 1) // HEAD_DIM
    for jr in range(rb):
        r = i * rb + jr
        rs = jnp.clip(r - NA_ROWS // 2, 0, grid_rows - NA_ROWS)
        start = pl.multiple_of((rs - (i - 1) * rb) * GRID_W, GRID_W)
        k = kband[pl.ds(start, band), :]
        v = vband[pl.ds(start, band), :]
        qs = _stack_masked(q_ref[jr * GRID_W:(jr + 1) * GRID_W, :], C_HEADS)
        s = lax.dot_general(qs, k, (((1,), (1,)), ((), ())), preferred_element_type=F32)
        s = s + bias_ref[r - rs].reshape(C_HEADS * GRID_W, band)
        m = jnp.max(s, axis=1, keepdims=True)
        p = jnp.exp2(s - m)
        l = jnp.sum(p, axis=1, keepdims=True)
        pv = jnp.dot(p.astype(BF16), v, preferred_element_type=F32) / l
        o = jnp.zeros((GRID_W, MIXER_WIDTH), F32)
        for hd in range(C_HEADS):
            o = jnp.where(lane_head == hd, pv[hd * GRID_W:(hd + 1) * GRID_W], o)
        o_ref[jr * GRID_W:(jr + 1) * GRID_W, :] = o.astype(o_ref.dtype)


def _nbr_call(q, k, v, bias):
    b, seq, w = q.shape
    grid_rows = seq // GRID_W
    rb = min(C_BLOCK_ROWS, grid_rows)
    blk = rb * GRID_W
    nblk = seq // blk
    cur = pl.BlockSpec((None, blk, w), lambda bi, i: (bi, i, 0))
    prev = pl.BlockSpec((None, blk, w), lambda bi, i: (bi, jnp.maximum(i - 1, 0), 0))
    nxt = pl.BlockSpec((None, blk, w), lambda bi, i: (bi, jnp.minimum(i + 1, nblk - 1), 0))
    return pl.pallas_call(
        functools.partial(_nbr_kernel, grid_rows=grid_rows),
        grid=(b, nblk),
        in_specs=[cur, prev, cur, nxt, prev, cur, nxt,
                  pl.BlockSpec(bias.shape, lambda bi, i: (0, 0, 0, 0))],
        out_specs=cur,
        out_shape=jax.ShapeDtypeStruct((b, seq, w), BF16),
        scratch_shapes=[pltpu.VMEM((3 * blk, w), BF16), pltpu.VMEM((3 * blk, w), BF16)],
        compiler_params=pltpu.CompilerParams(
            dimension_semantics=("parallel", "parallel"), vmem_limit_bytes=VMEM_LIMIT_BYTES),
        name="nbr_attention",
    )(q, k, k, k, v, v, v, bias)


def _nbr_bias_table(rpb):
    qc = np.arange(GRID_W)
    kc = np.arange(GRID_W)
    cs = np.clip(qc - NA_COLS // 2, 0, GRID_W - NA_COLS)
    valid = (kc[None, :] >= cs[:, None]) & (kc[None, :] < cs[:, None] + NA_COLS)
    dc = kc[None, :] - qc[:, None] + (NA_COLS - 1)
    d = np.arange(NA_ROWS)
    a = np.arange(NA_ROWS)
    dr = a[None, :] - d[:, None] + (NA_ROWS - 1)
    pick_r = (dr[:, :, None] == np.arange(2 * NA_ROWS - 1)).astype(np.float32)
    pick_c = ((dc[:, :, None] == np.arange(2 * NA_COLS - 1)) & valid[:, :, None]).astype(np.float32)
    tbl = jnp.einsum("hrc,dar,qkc->dhqak", rpb.astype(F32) * LOG2E, pick_r, pick_c,
                     precision=lax.Precision.HIGHEST)
    tbl = jnp.where(valid[None, None, :, None, :], tbl, NEG_BIG)
    return tbl.reshape(NA_ROWS, C_HEADS, GRID_W, NA_ROWS * GRID_W)


def _channel_kernel(h_ref, ya_ref, yb_ref, yc_ref, yd_ref, wout_ref, gffn_ref, wg_ref, wu_ref,
                    wd_ref, gple_ref, wpg_ref, p_ref, wpp_ref, gfin_ref, o_ref, *, final):
    w = MIXER_WIDTH
    h = h_ref[...]
    for j, y_ref in enumerate((ya_ref, yb_ref, yc_ref, yd_ref)):
        h = h + jnp.dot(y_ref[...], wout_ref[j * w:(j + 1) * w, :], preferred_element_type=F32)

    hn = _rms(h, gffn_ref[...]).astype(BF16)
    ffn = wg_ref.shape[1]
    acc = jnp.zeros(h.shape, F32)
    for c in range(ffn // FFN_CHUNK):
        cols = slice(c * FFN_CHUNK, (c + 1) * FFN_CHUNK)
        gate = jnp.dot(hn, wg_ref[:, cols], preferred_element_type=F32)
        up = jnp.dot(hn, wu_ref[:, cols], preferred_element_type=F32)
        act = (gate * jax.nn.sigmoid(gate) * up).astype(BF16)
        acc = acc + jnp.dot(act, wd_ref[cols, :], preferred_element_type=F32)
    h = h + acc

    hn = _rms(h, gple_ref[...]).astype(BF16)
    gate = jax.nn.sigmoid(jnp.dot(hn, wpg_ref[...], preferred_element_type=F32))
    emb = jnp.dot(p_ref[...].astype(BF16), wpp_ref[...], preferred_element_type=F32)
    h = h + gate * emb
    if final:
        h = _rms(h, gfin_ref[...])
    o_ref[...] = h


def _channel_call(h, ya, yb, yc, yd, wout, gffn, wg, wu, wd, gple, wpg, p, wpp, gfin, final, seq):
    rows, d = h.shape
    tm = min(ROW_TILE, seq)
    row_spec = lambda w: pl.BlockSpec((tm, w), lambda i: (i, 0))
    resident = lambda a: pl.BlockSpec(a.shape, lambda i: (0,) * a.ndim,
                                      pipeline_mode=pl.Buffered(1))
    return pl.pallas_call(
        functools.partial(_channel_kernel, final=final),
        grid=(rows // tm,),
        in_specs=[row_spec(d)] + [row_spec(MIXER_WIDTH)] * 4
                 + [resident(wout), resident(gffn), resident(wg), resident(wu), resident(wd),
                    resident(gple), resident(wpg), row_spec(p.shape[1]), resident(wpp),
                    resident(gfin)],
        out_specs=row_spec(d),
        out_shape=jax.ShapeDtypeStruct((rows, d), F32),
        compiler_params=pltpu.CompilerParams(
            dimension_semantics=("parallel",), vmem_limit_bytes=VMEM_LIMIT_BYTES),
        name="channel_mixers",
    )(h, ya, yb, yc, yd, wout, gffn, wg, wu, wd, gple, wpg, p, wpp, gfin)


def _rope_tables(seq):
    t = jnp.arange(seq)
    row = (t // GRID_W).astype(F32)
    col = (t % GRID_W).astype(F32)
    n_freq = HEAD_DIM // 4
    inv = ROPE_THETA ** (-jnp.arange(n_freq, dtype=F32) / n_freq)
    ang = jnp.concatenate([row[:, None] * inv, col[:, None] * inv], axis=-1)
    cos, sin = jnp.cos(ang), jnp.sin(ang)
    heads = LANES // HEAD_DIM
    return (jnp.tile(jnp.concatenate([cos, cos], axis=-1), (1, heads)),
            jnp.tile(jnp.concatenate([-sin, sin], axis=-1), (1, heads)))


def kernel(x, p, g_mix, w_in, a_q_norm, a_k_norm, b_lam_q, b_lam_k, b_sub_norm, c_rpb, d_ln_g,
           d_ln_b, d_w_s, d_b_s, w_out, g_ffn, w_gate, w_up, w_down, g_ple, w_ple_gate,
           w_ple_proj, g_final):
    b, seq, d = x.shape
    depth = w_in.shape[0]
    assert seq % GRID_W == 0 and seq // GRID_W >= NA_ROWS and seq % D_CHUNK == 0
    rows = b * seq
    h = x.reshape(rows, d)
    cos_t, sin_t = _rope_tables(seq)
    slopes = (2.0 ** (-8.0 / B_HEADS)) ** jnp.arange(1, B_HEADS + 1, dtype=F32) * LOG2E
    row2 = lambda a: a.reshape(1, -1).astype(F32)
    for i in range(depth):
        bs_tile = jnp.repeat(d_b_s[i].T, HEAD_DIM, axis=1)
        aqt, ak, avt, bqt, bk, bvt, cq, ck, cv, yd = _proj_call(
            h, row2(g_mix[i]), w_in[i].astype(BF16),
            row2(jnp.tile(a_q_norm[i], A_HEADS)), row2(jnp.tile(a_k_norm[i], A_KV_HEADS)),
            cos_t, sin_t, row2(d_ln_g[i]), row2(d_ln_b[i]), d_w_s[i].astype(BF16),
            bs_tile.astype(F32), seq)
        sh = lambda a: a.reshape(b, seq, MIXER_WIDTH)
        vt4 = lambda a: a.reshape(b, -1, a.shape[-2], a.shape[-1])
        ya = _gqa_call(vt4(aqt), sh(ak), vt4(avt))
        lam_init = 0.8 - 0.6 * math.exp(-0.3 * i)
        scal = jnp.concatenate([jnp.full((1,), lam_init, F32), slopes])
        yb = _diff_call(scal, vt4(bqt), sh(bk), vt4(bvt), b_lam_q[i].astype(F32),
                        b_lam_k[i].astype(F32), row2(jnp.tile(b_sub_norm[i], LANES // HEAD_DIM)))
        yc = _nbr_call(sh(cq), sh(ck), sh(cv), _nbr_bias_table(c_rpb[i]))
        fl = lambda a: a.reshape(rows, MIXER_WIDTH)
        h = _channel_call(
            h, fl(ya), fl(yb), fl(yc), yd, w_out[i].astype(BF16), row2(g_ffn[i]),
            w_gate[i].astype(BF16), w_up[i].astype(BF16), w_down[i].astype(BF16),
            row2(g_ple[i]), w_ple_gate[i].astype(BF16), p[i].reshape(rows, -1),
            w_ple_proj[i].astype(BF16), row2(g_final), i == depth - 1, seq)
    return h.reshape(b, seq, d)
```

```python
import functools
import math

import jax
import jax.numpy as jnp
import numpy as np
from jax import lax
from jax.experimental import pallas as pl
from jax.experimental.pallas import tpu as pltpu

F32 = jnp.float32
BF16 = jnp.bfloat16

HEAD_DIM = 64
GRID_W = 64
RMS_EPS = 1e-6
LN_EPS = 1e-5
A_HEADS = 4
A_KV_HEADS = 2
ROPE_THETA = 10000.0
B_HEADS = 4
B_QK_DIM = HEAD_DIM // 2
C_HEADS = 4
NA_ROWS = 8
NA_COLS = 16
D_GROUPS = 4
D_CHUNK = 128
MIXER_WIDTH = 256

LOG2E = math.log2(math.e)
NEG_BIG = -1e30

LANES = 128
BF16_SUBLANES = 16
VMEM_LIMIT_BYTES = 56 * 1024 * 1024

VT_ROWS = HEAD_DIM + BF16_SUBLANES

ROW_TILE = 512
A_Q_TILE = 256
B_Q_TILE = 128
C_BLOCK_ROWS = 8
FFN_CHUNK = 256
PIPELINE_UNROLL = 32
TILES_PER_STEP = 2
SCORE_SLOTS = 4
A_LOOKAHEAD = 2
B_LOOKAHEAD = 1


def _group_ones(width, group):
    r = lax.broadcasted_iota(jnp.int32, (width, width), 0) // group
    c = lax.broadcasted_iota(jnp.int32, (width, width), 1) // group
    return (r == c).astype(F32)


def _group_sum(x, group):
    return jnp.dot(x, _group_ones(x.shape[-1], group), precision=lax.Precision.HIGHEST,
                   preferred_element_type=F32)


def _rms(x, g):
    ms = jnp.mean(x * x, axis=-1, keepdims=True)
    return x * lax.rsqrt(ms + RMS_EPS) * g


def _proj_kernel(h_ref, gmix_ref, win_ref, gq_ref, gk_ref, cos_ref, sin_ref,
                 lng_ref, lnb_ref, ws_ref, bs_ref,
                 aqt_o, ak_o, avt_o, bqt_o, bk_o, bvt_o, cq_o, ck_o, cv_o, yd_o):
    tm = h_ref.shape[0]
    hn = _rms(h_ref[...], gmix_ref[...]).astype(BF16)

    def proj(lo, width):
        return jnp.dot(hn, win_ref[:, lo:lo + width], preferred_element_type=F32)

    cos = cos_ref[...]
    sin = sin_ref[...]

    def qk_norm_rope(y, gain, cos_t, sin_t):
        w = y.shape[-1]
        ss = _group_sum(y * y, HEAD_DIM)
        y = y * lax.rsqrt(ss * (1.0 / HEAD_DIM) + RMS_EPS) * gain
        lane = lax.broadcasted_iota(jnp.int32, y.shape, 1)
        half = HEAD_DIM // 2
        swapped = jnp.where(lane % HEAD_DIM < half,
                            pltpu.roll(y, w - half, axis=1), pltpu.roll(y, half, axis=1))
        return y * cos_t + swapped * sin_t

    a_scale = HEAD_DIM ** -0.5 * LOG2E
    cos2 = jnp.concatenate([cos, cos], axis=1)
    sin2 = jnp.concatenate([sin, sin], axis=1)
    aq = qk_norm_rope(proj(0, 256), gq_ref[...], cos2, sin2) * a_scale
    def store_tiles(o_ref, xt):
        tq = o_ref.shape[-1]
        for j in range(o_ref.shape[0]):
            o_ref[j] = xt[:, j * tq:(j + 1) * tq].astype(BF16)

    store_tiles(aqt_o, aq.T)
    ak = qk_norm_rope(proj(256, 128), gk_ref[...], cos, sin)
    av = proj(384, 128)
    lane128 = lax.broadcasted_iota(jnp.int32, (tm, LANES), 1)
    first = lane128 < HEAD_DIM

    def dup_heads(y):
        r = pltpu.roll(y, HEAD_DIM, axis=1)
        return jnp.concatenate([jnp.where(first, y, r), jnp.where(first, r, y)], axis=1)

    ak_o[...] = dup_heads(ak).astype(BF16)
    ones = jnp.ones((BF16_SUBLANES, tm), F32)
    avt = av.T
    avt_o[...] = jnp.concatenate(
        [avt[:HEAD_DIM], ones, avt[HEAD_DIM:], ones], axis=0).astype(BF16)

    store_tiles(bqt_o, (proj(512, 256) * (B_QK_DIM ** -0.5 * LOG2E)).T)
    bk = proj(768, 256)
    cc = lax.broadcasted_iota(jnp.int32, (tm, LANES), 0)
    flane = lax.broadcasted_iota(jnp.int32, (tm, LANES), 1)
    c_lo = cc % 256
    feat = jnp.where(flane < 3, c_lo, jnp.where(flane < 6, cc - c_lo, 0)).astype(F32)
    bk_o[...] = jnp.concatenate([bk[:, :LANES], feat, bk[:, LANES:], feat], axis=1).astype(BF16)
    bvt = proj(1024, 256).T
    bvt_o[...] = jnp.concatenate(
        [x for hd in range(B_HEADS) for x in (bvt[hd * HEAD_DIM:(hd + 1) * HEAD_DIM], ones)],
        axis=0).astype(BF16)

    cq_o[...] = (proj(1280, 256) * a_scale).astype(BF16)
    ck_o[...] = proj(1536, 256).astype(BF16)
    cv_o[...] = proj(1792, 256).astype(BF16)

    def gelu(t):
        return 0.5 * t * (1.0 + lax.erf(t * (2.0 ** -0.5)))

    u = gelu(proj(2048, 256))
    vv = gelu(proj(2304, 256))
    mu = jnp.mean(vv, axis=-1, keepdims=True)
    vc = vv - mu
    var = jnp.mean(vc * vc, axis=-1, keepdims=True)
    vv = (vc * lax.rsqrt(var + LN_EPS) * lng_ref[...] + lnb_ref[...]).astype(BF16)
    lane_group = lax.broadcasted_iota(jnp.int32, (D_CHUNK, MIXER_WIDTH), 1) // HEAD_DIM
    for n in range(tm // D_CHUNK):
        rows = slice(n * D_CHUNK, (n + 1) * D_CHUNK)
        vchunk = vv[rows, :]
        sv = bs_ref[...]
        for g in range(D_GROUPS):
            mixed = jnp.dot(ws_ref[g], vchunk, preferred_element_type=F32)
            sv = sv + jnp.where(lane_group == g, mixed, 0.0)
        yd_o[rows, :] = (u[rows, :] * sv).astype(BF16)


def _proj_call(h, gmix, win, gq, gk, cos_t, sin_t, lng, lnb, ws, bs, seq):
    rows, d = h.shape
    tm = min(ROW_TILE, seq)
    steps_per_seq = seq // tm
    row_spec = lambda w: pl.BlockSpec((tm, w), lambda i: (i, 0))
    full = lambda a: pl.BlockSpec(a.shape, lambda i: (0,) * a.ndim)
    tab_spec = pl.BlockSpec((tm, LANES), lambda i: (i % steps_per_seq, 0))
    row_out = (row_spec(MIXER_WIDTH), jax.ShapeDtypeStruct((rows, MIXER_WIDTH), BF16))
    def qt_out(tile):
        tq = min(tile, seq)
        return (pl.BlockSpec((tm // tq, MIXER_WIDTH, tq), lambda i: (i, 0, 0)),
                jax.ShapeDtypeStruct((rows // tq, MIXER_WIDTH, tq), BF16))
    def vt_out(heads):
        r = heads * VT_ROWS
        return (pl.BlockSpec((None, r, tm), lambda i: (i, 0, 0)),
                jax.ShapeDtypeStruct((rows // tm, r, tm), BF16))
    wide_out = (row_spec(2 * MIXER_WIDTH), jax.ShapeDtypeStruct((rows, 2 * MIXER_WIDTH), BF16))
    outs = ([qt_out(A_Q_TILE), row_out, vt_out(A_KV_HEADS),
             qt_out(B_Q_TILE), wide_out, vt_out(B_HEADS)] + [row_out] * 4)
    return pl.pallas_call(
        _proj_kernel,
        grid=(rows // tm,),
        in_specs=[row_spec(d), full(gmix), full(win), full(gq), full(gk), tab_spec, tab_spec,
                  full(lng), full(lnb), full(ws), full(bs)],
        out_specs=[o[0] for o in outs],
        out_shape=[o[1] for o in outs],
        compiler_params=pltpu.CompilerParams(
            dimension_semantics=("parallel",), vmem_limit_bytes=VMEM_LIMIT_BYTES),
        name="proj_mixers",
    )(h, gmix, win, gq, gk, cos_t, sin_t, lng, lnb, ws, bs)


def _online_softmax_step(s, m_cur, vts, m_scr, acc_scr, col_offset=None):
    m_prev = m_scr[...]
    if col_offset is not None:
        m_cur = m_cur - col_offset
    m_next = jnp.maximum(m_prev, m_cur)
    alpha = jnp.exp2(m_prev - m_next)
    shift = m_next if col_offset is None else m_next + col_offset
    p = jnp.exp2(s - shift).astype(BF16)
    width = p.shape[1] // len(vts)
    pv = [jnp.dot(vt, p[:, j * width:(j + 1) * width], preferred_element_type=F32)
          for j, vt in enumerate(vts)]
    acc_scr[...] = alpha * acc_scr[...] + jnp.concatenate(pv, axis=1)
    m_scr[...] = m_next


def _stack_masked_t(qt, groups):
    qt = qt.astype(F32)
    width = qt.shape[0] // groups
    row = lax.broadcasted_iota(jnp.int32, qt.shape, 0) // width
    zero = jnp.zeros_like(qt)
    return jnp.concatenate([jnp.where(row == j, qt, zero) for j in range(groups)],
                           axis=1).astype(BF16)


def _stack_masked(q, groups):
    width = q.shape[1] // groups
    lane = lax.broadcasted_iota(jnp.int32, q.shape, 1) // width
    zero = jnp.zeros_like(q)
    return jnp.concatenate([jnp.where(lane == j, q, zero) for j in range(groups)], axis=0)


def _init_softmax_state(m_scr, acc_scr):
    m_scr[...] = jnp.full(m_scr.shape, NEG_BIG, F32)
    acc_scr[...] = jnp.zeros(acc_scr.shape, F32)


def _normalized(acc_scr):
    return acc_scr[:HEAD_DIM, :] / acc_scr[HEAD_DIM:HEAD_DIM + 1, :]


def _dense_specs(b, seq, tq, nk, vt_rows, tk, k_width=LANES):
    qt_spec = pl.BlockSpec((None, seq // tq, LANES, tq), lambda bi, g, qi: (bi, 0, g, 0))
    k_spec = pl.BlockSpec((None, seq, k_width), lambda bi, g, qi: (bi, 0, g))
    vt_spec = pl.BlockSpec((None, nk, vt_rows, tk), lambda bi, g, qi: (bi, 0, g, 0))
    o_spec = pl.BlockSpec((None, tq * TILES_PER_STEP, LANES), lambda bi, g, qi: (bi, qi, g))
    return qt_spec, k_spec, vt_spec, o_spec


def _pipelined_chunks(nk, lookahead, first_tile, scores, consume, s_scr, max_scr):
    unroll = PIPELINE_UNROLL if nk % PIPELINE_UNROLL == 0 else s_scr.shape[0]
    slots = s_scr.shape[0]
    assert nk % unroll == 0 and unroll % slots == 0 and lookahead < slots <= nk

    def produce(ahead, c, slot):
        s = scores(ahead, c)
        s_scr[slot] = s
        max_scr[slot] = jnp.max(s, axis=0, keepdims=True)

    @pl.when(first_tile)
    def _():
        for c in range(lookahead):
            produce(0, c, c)

    def body(i, carry):
        for j in range(unroll):
            c = unroll * i + j
            ahead = c + lookahead
            if j + lookahead < unroll:
                produce(0, ahead, (j + lookahead) % slots)
            else:
                wrap = int(ahead >= nk) if isinstance(ahead, int) else (ahead >= nk).astype(jnp.int32)
                produce(wrap, ahead - wrap * nk, (j + lookahead) % slots)
            consume(c, s_scr[j % slots], max_scr[j % slots])
        return carry

    if nk == unroll:
        body(0, 0)
    else:
        lax.fori_loop(0, nk // unroll, body, 0)


def _stage_query_tiles(qt_ref, qs_scr, qi, groups, extend=lambda stack: stack):
    last = qt_ref.shape[0] - 1

    @pl.when(qi == 0)
    def _():
        qs_scr[0] = extend(_stack_masked_t(qt_ref[0], groups))

    nxt = jnp.minimum(qi + 1, last)
    qs_scr[(qi + 1) % 2] = extend(_stack_masked_t(qt_ref[nxt], groups))
    return nxt


def _gqa_kernel(qt_ref, k_ref, vt_ref, o_ref, qs_scr, s_scr, max_scr, m_scr, acc_scr):
    tq = qt_ref.shape[2]
    for t in range(TILES_PER_STEP):
        _gqa_tile(pl.program_id(2) * TILES_PER_STEP + t, qt_ref, k_ref, vt_ref,
                  o_ref.at[t * tq:(t + 1) * tq, :], qs_scr, s_scr, max_scr, m_scr, acc_scr)


def _gqa_tile(qi, qt_ref, k_ref, vt_ref, o_ref, qs_scr, s_scr, max_scr, m_scr, acc_scr):
    tq = qt_ref.shape[2]
    nk, _, tk = vt_ref.shape
    _stage_query_tiles(qt_ref, qs_scr, qi, 2)
    _init_softmax_state(m_scr, acc_scr)

    def scores(ahead, c):
        start = pl.multiple_of(c * tk, tk)
        return jnp.dot(k_ref[pl.ds(start, tk), :], qs_scr[(qi + ahead) % 2],
                       preferred_element_type=F32)

    def consume(c, s, m_cur):
        _online_softmax_step(s, m_cur, [vt_ref[c]], m_scr, acc_scr)

    _pipelined_chunks(nk, A_LOOKAHEAD, qi == 0, scores, consume, s_scr, max_scr)
    o = _normalized(acc_scr)
    o_ref[...] = jnp.concatenate([o[:, :tq], o[:, tq:]], axis=0).T.astype(o_ref.dtype)


def _gqa_call(qt, k, vt):
    b, nk, vt_rows, tk = vt.shape
    vt_rows //= A_KV_HEADS
    seq = k.shape[1]
    tq = min(A_Q_TILE, seq)
    qt_spec, k_spec, vt_spec, o_spec = _dense_specs(b, seq, tq, nk, vt_rows, tk)
    return pl.pallas_call(
        _gqa_kernel,
        grid=(b, A_KV_HEADS, seq // (tq * TILES_PER_STEP)),
        in_specs=[qt_spec, k_spec, vt_spec],
        out_specs=o_spec,
        out_shape=jax.ShapeDtypeStruct((b, seq, MIXER_WIDTH), BF16),
        scratch_shapes=[pltpu.VMEM((2, LANES, 2 * tq), BF16),
                        pltpu.VMEM((SCORE_SLOTS, tk, 2 * tq), F32),
                        pltpu.VMEM((SCORE_SLOTS, 1, 2 * tq), F32),
                        pltpu.VMEM((1, 2 * tq), F32),
                        pltpu.VMEM((VT_ROWS, 2 * tq), F32)],
        compiler_params=pltpu.CompilerParams(
            dimension_semantics=("parallel", "parallel", "arbitrary"),
            vmem_limit_bytes=VMEM_LIMIT_BYTES),
        name="gqa_attention",
    )(qt, k, vt)


def _diff_kernel(scal_ref, qt_ref, k_ref, vt_ref, lq_ref, lk_ref, gsub_ref, o_ref,
                 qs_scr, s_scr, max_scr, key_scr, m_scr, acc_scr):
    tq = qt_ref.shape[2]
    for t in range(TILES_PER_STEP):
        _diff_tile(pl.program_id(2) * TILES_PER_STEP + t, scal_ref, qt_ref, k_ref, vt_ref,
                   lq_ref, lk_ref, gsub_ref, o_ref.at[t * tq:(t + 1) * tq, :],
                   qs_scr, s_scr, max_scr, key_scr, m_scr, acc_scr)


def _diff_tile(qi, scal_ref, qt_ref, k_ref, vt_ref, lq_ref, lk_ref, gsub_ref, o_ref,
               qs_scr, s_scr, max_scr, key_scr, m_scr, acc_scr):
    tq = qt_ref.shape[2]
    nk, _, tk = vt_ref.shape
    pair = pl.program_id(1)
    lam_init = scal_ref[0]
    slope_a = scal_ref[1 + 2 * pair]
    slope_b = scal_ref[2 + 2 * pair]
    q0 = qi * tq
    c_mid = q0 // tk
    col = lax.broadcasted_iota(jnp.int32, (1, 4 * tq), 1)
    col_slope = jnp.where(col < 2 * tq, slope_a, slope_b)
    col_pos = (q0 + col % tq).astype(F32)
    tiles_per_chunk = tk // tq

    s1 = col_slope.astype(BF16).astype(F32)
    s2 = (col_slope - s1).astype(BF16).astype(F32)
    s3 = (col_slope - s1 - s2).astype(BF16).astype(F32)
    frow = lax.broadcasted_iota(jnp.int32, (LANES, 4 * tq), 0)
    slope_rows = jnp.where(frow % 3 == 0, s1, jnp.where(frow % 3 == 1, s2, s3))
    slope_rows = jnp.where(frow < 6, slope_rows, 0.0)

    def extend(stack):
        return jnp.stack([jnp.concatenate([stack, (sg * slope_rows).astype(BF16)], axis=0)
                          for sg in (1.0, -1.0, 0.0)])

    nxt = _stage_query_tiles(qt_ref, qs_scr, qi, 4, extend)
    _init_softmax_state(m_scr, acc_scr)

    @pl.when(qi == 0)
    def _():
        rel = (lax.broadcasted_iota(jnp.int32, (tk, 4 * tq), 1) % tq
               - lax.broadcasted_iota(jnp.int32, (tk, 4 * tq), 0))
        for t in range(tiles_per_chunk):
            key_scr[t] = -(jnp.abs(rel + t * tq).astype(F32) * col_slope)

    def scores(ahead, j):
        assert isinstance(j, int)
        tile = jnp.where(ahead == 0, qi, nxt)
        mid = tile // tiles_per_chunk
        c = (mid + j) % nk
        variant = 2 if j == 0 else jnp.where(c < mid, 0, 1)
        start = pl.multiple_of(c * tk, tk)
        s = jnp.dot(k_ref[pl.ds(start, tk), :], qs_scr[(qi + ahead) % 2, variant],
                    preferred_element_type=F32)
        return s + key_scr[tile % tiles_per_chunk] if j == 0 else s

    def consume(j, s, m_cur):
        c = (c_mid + j) % nk
        vts = [vt_ref[c, :VT_ROWS, :], vt_ref[c, VT_ROWS:, :]]
        if j == 0:
            col_offset = None
        else:
            sign = jnp.where(c < c_mid, 1.0, -1.0).astype(F32)
            col_offset = col_slope * (sign * (col_pos - (c * tk).astype(F32)))
        _online_softmax_step(s, m_cur, vts, m_scr, acc_scr, col_offset)

    _pipelined_chunks(nk, B_LOOKAHEAD, qi == 0, scores, consume, s_scr, max_scr)

    t = jnp.exp(jnp.sum(lq_ref[...] * lk_ref[...], axis=1, keepdims=True))
    lam = t[0:1, :] - t[1:2, :] + lam_init
    o = _normalized(acc_scr)
    o_a = o[:, 0:tq] - lam * o[:, tq:2 * tq]
    o_b = o[:, 2 * tq:3 * tq] - lam * o[:, 3 * tq:]
    def head_rms(y):
        return y * lax.rsqrt(jnp.mean(y * y, axis=0, keepdims=True) + RMS_EPS)

    o = jnp.concatenate([head_rms(o_a), head_rms(o_b)], axis=0).T
    o_ref[...] = (o * gsub_ref[...] * (1.0 - lam_init)).astype(o_ref.dtype)


def _diff_call(scal, qt, k, vt, lq, lk, gsub):
    b, nk, vt_rows, tk = vt.shape
    vt_rows //= B_HEADS // 2
    seq = k.shape[1]
    tq = min(B_Q_TILE, seq)
    assert tk % tq == 0
    qt_spec, k_spec, vt_spec, o_spec = _dense_specs(b, seq, tq, nk, vt_rows, tk, 2 * LANES)
    full = lambda a: pl.BlockSpec(a.shape, lambda bi, g, qi: (0,) * a.ndim)
    return pl.pallas_call(
        _diff_kernel,
        grid=(b, B_HEADS // 2, seq // (tq * TILES_PER_STEP)),
        in_specs=[pl.BlockSpec(memory_space=pltpu.SMEM), qt_spec, k_spec, vt_spec,
                  full(lq), full(lk), full(gsub)],
        out_specs=o_spec,
        out_shape=jax.ShapeDtypeStruct((b, seq, MIXER_WIDTH), BF16),
        scratch_shapes=[pltpu.VMEM((2, 3, 2 * LANES, 4 * tq), BF16),
                        pltpu.VMEM((SCORE_SLOTS, tk, 4 * tq), F32),
                        pltpu.VMEM((SCORE_SLOTS, 1, 4 * tq), F32),
                        pltpu.VMEM((tk // tq, tk, 4 * tq), F32),
                        pltpu.VMEM((1, 4 * tq), F32),
                        pltpu.VMEM((VT_ROWS, 4 * tq), F32)],
        compiler_params=pltpu.CompilerParams(
            dimension_semantics=("parallel", "parallel", "arbitrary"),
            vmem_limit_bytes=VMEM_LIMIT_BYTES),
        name="diff_attention",
    )(scal, qt, k, vt, lq, lk, gsub)


def _nbr_kernel(q_ref, kp_ref, kc_ref, kn_ref, vp_ref, vc_ref, vn_ref, bias_ref, o_ref,
                kband, vband, *, grid_rows):
    blk = q_ref.shape[0]
    rb = blk // GRID_W
    i = pl.program_id(1)
    kband[0:blk] = kp_ref[...]
    kband[blk:2 * blk] = kc_ref[...]
    kband[2 * blk:] = kn_ref[...]
    vband[0:blk] = vp_ref[...]
    vband[blk:2 * blk] = vc_ref[...]
    vband[2 * blk:] = vn_ref[...]
    band = NA_ROWS * GRID_W
    lane_head = lax.broadcasted_iota(jnp.int32, (GRID_W, MIXER_WIDTH), 1) // HEAD_DIM
    for jr in range(rb):
        r = i * rb + jr
        rs = jnp.clip(r - NA_ROWS // 2, 0, grid_rows - NA_ROWS)
        start = pl.multiple_of((rs - (i - 1) * rb) * GRID_W, GRID_W)
        k = kband[pl.ds(start, band), :]
        v = vband[pl.ds(start, band), :]
        qs = _stack_masked(q_ref[jr * GRID_W:(jr + 1) * GRID_W, :], C_HEADS)
        s = lax.dot_general(qs, k, (((1,), (1,)), ((), ())), preferred_element_type=F32)
        s = s + bias_ref[r - rs].reshape(C_HEADS * GRID_W, band)
        m = jnp.max(s, axis=1, keepdims=True)
        p = jnp.exp2(s - m)
        l = jnp.sum(p, axis=1, keepdims=True)
        pv = jnp.dot(p.astype(BF16), v, preferred_element_type=F32) / l
        o = jnp.zeros((GRID_W, MIXER_WIDTH), F32)
        for hd in range(C_HEADS):
            o = jnp.where(lane_head == hd, pv[hd * GRID_W:(hd + 1) * GRID_W], o)
        o_ref[jr * GRID_W:(jr + 1) * GRID_W, :] = o.astype(o_ref.dtype)


def _nbr_call(q, k, v, bias):
    b, seq, w = q.shape
    grid_rows = seq // GRID_W
    rb = min(C_BLOCK_ROWS, grid_rows)
    blk = rb * GRID_W
    nblk = seq // blk
    cur = pl.BlockSpec((None, blk, w), lambda bi, i: (bi, i, 0))
    prev = pl.BlockSpec((None, blk, w), lambda bi, i: (bi, jnp.maximum(i - 1, 0), 0))
    nxt = pl.BlockSpec((None, blk, w), lambda bi, i: (bi, jnp.minimum(i + 1, nblk - 1), 0))
    return pl.pallas_call(
        functools.partial(_nbr_kernel, grid_rows=grid_rows),
        grid=(b, nblk),
        in_specs=[cur, prev, cur, nxt, prev, cur, nxt,
                  pl.BlockSpec(bias.shape, lambda bi, i: (0, 0, 0, 0))],
        out_specs=cur,
        out_shape=jax.ShapeDtypeStruct((b, seq, w), BF16),
        scratch_shapes=[pltpu.VMEM((3 * blk, w), BF16), pltpu.VMEM((3 * blk, w), BF16)],
        compiler_params=pltpu.CompilerParams(
            dimension_semantics=("parallel", "parallel"), vmem_limit_bytes=VMEM_LIMIT_BYTES),
        name="nbr_attention",
    )(q, k, k, k, v, v, v, bias)


def _nbr_bias_table(rpb):
    qc = np.arange(GRID_W)
    kc = np.arange(GRID_W)
    cs = np.clip(qc - NA_COLS // 2, 0, GRID_W - NA_COLS)
    valid = (kc[None, :] >= cs[:, None]) & (kc[None, :] < cs[:, None] + NA_COLS)
    dc = kc[None, :] - qc[:, None] + (NA_COLS - 1)
    d = np.arange(NA_ROWS)
    a = np.arange(NA_ROWS)
    dr = a[None, :] - d[:, None] + (NA_ROWS - 1)
    pick_r = (dr[:, :, None] == np.arange(2 * NA_ROWS - 1)).astype(np.float32)
    pick_c = ((dc[:, :, None] == np.arange(2 * NA_COLS - 1)) & valid[:, :, None]).astype(np.float32)
    tbl = jnp.einsum("hrc,dar,qkc->dhqak", rpb.astype(F32) * LOG2E, pick_r, pick_c,
                     precision=lax.Precision.HIGHEST)
    tbl = jnp.where(valid[None, None, :, None, :], tbl, NEG_BIG)
    return tbl.reshape(NA_ROWS, C_HEADS, GRID_W, NA_ROWS * GRID_W)


def _channel_kernel(h_ref, ya_ref, yb_ref, yc_ref, yd_ref, wout_ref, gffn_ref, wg_ref, wu_ref,
                    wd_ref, gple_ref, wpg_ref, p_ref, wpp_ref, gfin_ref, o_ref, *, final):
    w = MIXER_WIDTH
    h = h_ref[...]
    for j, y_ref in enumerate((ya_ref, yb_ref, yc_ref, yd_ref)):
        h = h + jnp.dot(y_ref[...], wout_ref[j * w:(j + 1) * w, :], preferred_element_type=F32)

    hn = _rms(h, gffn_ref[...]).astype(BF16)
    ffn = wg_ref.shape[1]
    acc = jnp.zeros(h.shape, F32)
    for c in range(ffn // FFN_CHUNK):
        cols = slice(c * FFN_CHUNK, (c + 1) * FFN_CHUNK)
        gate = jnp.dot(hn, wg_ref[:, cols], preferred_element_type=F32)
        up = jnp.dot(hn, wu_ref[:, cols], preferred_element_type=F32)
        act = (gate * jax.nn.sigmoid(gate) * up).astype(BF16)
        acc = acc + jnp.dot(act, wd_ref[cols, :], preferred_element_type=F32)
    h = h + acc

    hn = _rms(h, gple_ref[...]).astype(BF16)
    gate = jax.nn.sigmoid(jnp.dot(hn, wpg_ref[...], preferred_element_type=F32))
    emb = jnp.dot(p_ref[...].astype(BF16), wpp_ref[...], preferred_element_type=F32)
    h = h + gate * emb
    if final:
        h = _rms(h, gfin_ref[...])
    o_ref[...] = h


def _channel_call(h, ya, yb, yc, yd, wout, gffn, wg, wu, wd, gple, wpg, p, wpp, gfin, final, seq):
    rows, d = h.shape
    tm = min(ROW_TILE, seq)
    row_spec = lambda w: pl.BlockSpec((tm, w), lambda i: (i, 0))
    resident = lambda a: pl.BlockSpec(a.shape, lambda i: (0,) * a.ndim,
                                      pipeline_mode=pl.Buffered(1))
    return pl.pallas_call(
        functools.partial(_channel_kernel, final=final),
        grid=(rows // tm,),
        in_specs=[row_spec(d)] + [row_spec(MIXER_WIDTH)] * 4
                 + [resident(wout), resident(gffn), resident(wg), resident(wu), resident(wd),
                    resident(gple), resident(wpg), row_spec(p.shape[1]), resident(wpp),
                    resident(gfin)],
        out_specs=row_spec(d),
        out_shape=jax.ShapeDtypeStruct((rows, d), F32),
        compiler_params=pltpu.CompilerParams(
            dimension_semantics=("parallel",), vmem_limit_bytes=VMEM_LIMIT_BYTES),
        name="channel_mixers",
    )(h, ya, yb, yc, yd, wout, gffn, wg, wu, wd, gple, wpg, p, wpp, gfin)


def _rope_tables(seq):
    t = jnp.arange(seq)
    row = (t // GRID_W).astype(F32)
    col = (t % GRID_W).astype(F32)
    n_freq = HEAD_DIM // 4
    inv = ROPE_THETA ** (-jnp.arange(n_freq, dtype=F32) / n_freq)
    ang = jnp.concatenate([row[:, None] * inv, col[:, None] * inv], axis=-1)
    cos, sin = jnp.cos(ang), jnp.sin(ang)
    heads = LANES // HEAD_DIM
    return (jnp.tile(jnp.concatenate([cos, cos], axis=-1), (1, heads)),
            jnp.tile(jnp.concatenate([-sin, sin], axis=-1), (1, heads)))


def kernel(x, p, g_mix, w_in, a_q_norm, a_k_norm, b_lam_q, b_lam_k, b_sub_norm, c_rpb, d_ln_g,
           d_ln_b, d_w_s, d_b_s, w_out, g_ffn, w_gate, w_up, w_down, g_ple, w_ple_gate,
           w_ple_proj, g_final):
    b, seq, d = x.shape
    depth = w_in.shape[0]
    assert seq % GRID_W == 0 and seq // GRID_W >= NA_ROWS and seq % D_CHUNK == 0
    rows = b * seq
    h = x.reshape(rows, d)
    cos_t, sin_t = _rope_tables(seq)
    slopes = (2.0 ** (-8.0 / B_HEADS)) ** jnp.arange(1, B_HEADS + 1, dtype=F32) * LOG2E
    row2 = lambda a: a.reshape(1, -1).astype(F32)
    for i in range(depth):
        bs_tile = jnp.repeat(d_b_s[i].T, HEAD_DIM, axis=1)
        aqt, ak, avt, bqt, bk, bvt, cq, ck, cv, yd = _proj_call(
            h, row2(g_mix[i]), w_in[i].astype(BF16),
            row2(jnp.tile(a_q_norm[i], A_HEADS)), row2(jnp.tile(a_k_norm[i], A_KV_HEADS)),
            cos_t, sin_t, row2(d_ln_g[i]), row2(d_ln_b[i]), d_w_s[i].astype(BF16),
            bs_tile.astype(F32), seq)
        sh = lambda a: a.reshape(b, seq, MIXER_WIDTH)
        vt4 = lambda a: a.reshape(b, -1, a.shape[-2], a.shape[-1])
        ya = _gqa_call(vt4(aqt), sh(ak), vt4(avt))
        lam_init = 0.8 - 0.6 * math.exp(-0.3 * i)
        scal = jnp.concatenate([jnp.full((1,), lam_init, F32), slopes])
        yb = _diff_call(scal, vt4(bqt), bk.reshape(b, seq, -1), vt4(bvt), b_lam_q[i].astype(F32),
                        b_lam_k[i].astype(F32), row2(jnp.tile(b_sub_norm[i], LANES // HEAD_DIM)))
        yc = _nbr_call(sh(cq), sh(ck), sh(cv), _nbr_bias_table(c_rpb[i]))
        fl = lambda a: a.reshape(rows, MIXER_WIDTH)
        h = _channel_call(
            h, fl(ya), fl(yb), fl(yc), yd, w_out[i].astype(BF16), row2(g_ffn[i]),
            w_gate[i].astype(BF16), w_up[i].astype(BF16), w_down[i].astype(BF16),
            row2(g_ple[i]), w_ple_gate[i].astype(BF16), p[i].reshape(rows, -1),
            w_ple_proj[i].astype(BF16), row2(g_final), i == depth - 1, seq)
    return h.reshape(b, seq, d)
```

```python
import functools
import math

import jax
import jax.numpy as jnp
import numpy as np
from jax import lax
from jax.experimental import pallas as pl
from jax.experimental.pallas import tpu as pltpu

F32 = jnp.float32
BF16 = jnp.bfloat16

HEAD_DIM = 64
GRID_W = 64
RMS_EPS = 1e-6
LN_EPS = 1e-5
A_HEADS = 4
A_KV_HEADS = 2
ROPE_THETA = 10000.0
B_HEADS = 4
B_QK_DIM = HEAD_DIM // 2
C_HEADS = 4
NA_ROWS = 8
NA_COLS = 16
D_GROUPS = 4
D_CHUNK = 128
MIXER_WIDTH = 256

LOG2E = math.log2(math.e)
NEG_BIG = -1e30

LANES = 128
BF16_SUBLANES = 16
VMEM_LIMIT_BYTES = 56 * 1024 * 1024

VT_ROWS = HEAD_DIM + BF16_SUBLANES

ROW_TILE = 512
A_Q_TILE = 256
B_Q_TILE = 128
C_BLOCK_ROWS = 8
FFN_CHUNK = 256
PIPELINE_UNROLL = 32
TILES_PER_STEP = 2
SCORE_SLOTS = 4
A_LOOKAHEAD = 2
B_LOOKAHEAD = 2


def _group_ones(width, group):
    r = lax.broadcasted_iota(jnp.int32, (width, width), 0) // group
    c = lax.broadcasted_iota(jnp.int32, (width, width), 1) // group
    return (r == c).astype(F32)


def _group_sum(x, group):
    return jnp.dot(x, _group_ones(x.shape[-1], group), precision=lax.Precision.HIGHEST,
                   preferred_element_type=F32)


def _rms(x, g):
    ms = jnp.mean(x * x, axis=-1, keepdims=True)
    return x * lax.rsqrt(ms + RMS_EPS) * g


def _proj_kernel(h_ref, gmix_ref, win_ref, gq_ref, gk_ref, cos_ref, sin_ref,
                 lng_ref, lnb_ref, ws_ref, bs_ref,
                 aqt_o, ak_o, avt_o, bqt_o, bk_o, bvt_o, cq_o, ck_o, cv_o, yd_o):
    tm = h_ref.shape[0]
    hn = _rms(h_ref[...], gmix_ref[...]).astype(BF16)

    def proj(lo, width):
        return jnp.dot(hn, win_ref[:, lo:lo + width], preferred_element_type=F32)

    cos = cos_ref[...]
    sin = sin_ref[...]

    def qk_norm_rope(y, gain, cos_t, sin_t):
        w = y.shape[-1]
        ss = _group_sum(y * y, HEAD_DIM)
        y = y * lax.rsqrt(ss * (1.0 / HEAD_DIM) + RMS_EPS) * gain
        lane = lax.broadcasted_iota(jnp.int32, y.shape, 1)
        half = HEAD_DIM // 2
        swapped = jnp.where(lane % HEAD_DIM < half,
                            pltpu.roll(y, w - half, axis=1), pltpu.roll(y, half, axis=1))
        return y * cos_t + swapped * sin_t

    a_scale = HEAD_DIM ** -0.5 * LOG2E
    cos2 = jnp.concatenate([cos, cos], axis=1)
    sin2 = jnp.concatenate([sin, sin], axis=1)
    aq = qk_norm_rope(proj(0, 256), gq_ref[...], cos2, sin2) * a_scale
    def store_tiles(o_ref, xt):
        tq = o_ref.shape[-1]
        for j in range(o_ref.shape[0]):
            o_ref[j] = xt[:, j * tq:(j + 1) * tq].astype(BF16)

    store_tiles(aqt_o, aq.T)
    ak = qk_norm_rope(proj(256, 128), gk_ref[...], cos, sin)
    av = proj(384, 128)
    lane128 = lax.broadcasted_iota(jnp.int32, (tm, LANES), 1)
    first = lane128 < HEAD_DIM

    def dup_heads(y):
        r = pltpu.roll(y, HEAD_DIM, axis=1)
        return jnp.concatenate([jnp.where(first, y, r), jnp.where(first, r, y)], axis=1)

    ak_o[...] = dup_heads(ak).astype(BF16)
    ones = jnp.ones((BF16_SUBLANES, tm), F32)
    avt = av.T
    avt_o[...] = jnp.concatenate(
        [avt[:HEAD_DIM], ones, avt[HEAD_DIM:], ones], axis=0).astype(BF16)

    store_tiles(bqt_o, (proj(512, 256) * (B_QK_DIM ** -0.5 * LOG2E)).T)
    bk = proj(768, 256)
    cc = lax.broadcasted_iota(jnp.int32, (tm, LANES), 0)
    flane = lax.broadcasted_iota(jnp.int32, (tm, LANES), 1)
    c_lo = cc % 256
    feat = jnp.where(flane < 3, c_lo, jnp.where(flane < 6, cc - c_lo, 0)).astype(F32)
    bk_o[...] = jnp.concatenate([bk[:, :LANES], feat, bk[:, LANES:], feat], axis=1).astype(BF16)
    bvt = proj(1024, 256).T
    bvt_o[...] = jnp.concatenate(
        [x for hd in range(B_HEADS) for x in (bvt[hd * HEAD_DIM:(hd + 1) * HEAD_DIM], ones)],
        axis=0).astype(BF16)

    cq_o[...] = (proj(1280, 256) * a_scale).astype(BF16)
    ck_o[...] = proj(1536, 256).astype(BF16)
    cv_o[...] = proj(1792, 256).astype(BF16)

    def gelu(t):
        return 0.5 * t * (1.0 + lax.erf(t * (2.0 ** -0.5)))

    u = gelu(proj(2048, 256))
    vv = gelu(proj(2304, 256))
    mu = jnp.mean(vv, axis=-1, keepdims=True)
    vc = vv - mu
    var = jnp.mean(vc * vc, axis=-1, keepdims=True)
    vv = (vc * lax.rsqrt(var + LN_EPS) * lng_ref[...] + lnb_ref[...]).astype(BF16)
    lane_group = lax.broadcasted_iota(jnp.int32, (D_CHUNK, MIXER_WIDTH), 1) // HEAD_DIM
    for n in range(tm // D_CHUNK):
        rows = slice(n * D_CHUNK, (n + 1) * D_CHUNK)
        vchunk = vv[rows, :]
        sv = bs_ref[...]
        for g in range(D_GROUPS):
            mixed = jnp.dot(ws_ref[g], vchunk, preferred_element_type=F32)
            sv = sv + jnp.where(lane_group == g, mixed, 0.0)
        yd_o[rows, :] = (u[rows, :] * sv).astype(BF16)


def _proj_call(h, gmix, win, gq, gk, cos_t, sin_t, lng, lnb, ws, bs, seq):
    rows, d = h.shape
    tm = min(ROW_TILE, seq)
    steps_per_seq = seq // tm
    row_spec = lambda w: pl.BlockSpec((tm, w), lambda i: (i, 0))
    full = lambda a: pl.BlockSpec(a.shape, lambda i: (0,) * a.ndim)
    tab_spec = pl.BlockSpec((tm, LANES), lambda i: (i % steps_per_seq, 0))
    row_out = (row_spec(MIXER_WIDTH), jax.ShapeDtypeStruct((rows, MIXER_WIDTH), BF16))
    def qt_out(tile):
        tq = min(tile, seq)
        return (pl.BlockSpec((tm // tq, MIXER_WIDTH, tq), lambda i: (i, 0, 0)),
                jax.ShapeDtypeStruct((rows // tq, MIXER_WIDTH, tq), BF16))
    def vt_out(heads):
        r = heads * VT_ROWS
        return (pl.BlockSpec((None, r, tm), lambda i: (i, 0, 0)),
                jax.ShapeDtypeStruct((rows // tm, r, tm), BF16))
    wide_out = (row_spec(2 * MIXER_WIDTH), jax.ShapeDtypeStruct((rows, 2 * MIXER_WIDTH), BF16))
    outs = ([qt_out(A_Q_TILE), row_out, vt_out(A_KV_HEADS),
             qt_out(B_Q_TILE), wide_out, vt_out(B_HEADS)] + [row_out] * 4)
    return pl.pallas_call(
        _proj_kernel,
        grid=(rows // tm,),
        in_specs=[row_spec(d), full(gmix), full(win), full(gq), full(gk), tab_spec, tab_spec,
                  full(lng), full(lnb), full(ws), full(bs)],
        out_specs=[o[0] for o in outs],
        out_shape=[o[1] for o in outs],
        compiler_params=pltpu.CompilerParams(
            dimension_semantics=("parallel",), vmem_limit_bytes=VMEM_LIMIT_BYTES),
        name="proj_mixers",
    )(h, gmix, win, gq, gk, cos_t, sin_t, lng, lnb, ws, bs)


def _online_softmax_step(s, m_cur, vts, m_scr, acc_scr, col_offset=None):
    m_prev = m_scr[...]
    if col_offset is not None:
        m_cur = m_cur - col_offset
    m_next = jnp.maximum(m_prev, m_cur)
    alpha = jnp.exp2(m_prev - m_next)
    shift = m_next if col_offset is None else m_next + col_offset
    p = jnp.exp2(s - shift).astype(BF16)
    width = p.shape[1] // len(vts)
    pv = [jnp.dot(vt, p[:, j * width:(j + 1) * width], preferred_element_type=F32)
          for j, vt in enumerate(vts)]
    acc_scr[...] = alpha * acc_scr[...] + jnp.concatenate(pv, axis=1)
    m_scr[...] = m_next


def _stack_masked_t(qt, groups):
    qt = qt.astype(F32)
    width = qt.shape[0] // groups
    row = lax.broadcasted_iota(jnp.int32, qt.shape, 0) // width
    zero = jnp.zeros_like(qt)
    return jnp.concatenate([jnp.where(row == j, qt, zero) for j in range(groups)],
                           axis=1).astype(BF16)


def _stack_masked(q, groups):
    width = q.shape[1] // groups
    lane = lax.broadcasted_iota(jnp.int32, q.shape, 1) // width
    zero = jnp.zeros_like(q)
    return jnp.concatenate([jnp.where(lane == j, q, zero) for j in range(groups)], axis=0)


def _init_softmax_state(m_scr, acc_scr):
    m_scr[...] = jnp.full(m_scr.shape, NEG_BIG, F32)
    acc_scr[...] = jnp.zeros(acc_scr.shape, F32)


def _normalized(acc_scr):
    return acc_scr[:HEAD_DIM, :] / acc_scr[HEAD_DIM:HEAD_DIM + 1, :]


def _dense_specs(b, seq, tq, nk, vt_rows, tk, k_width=LANES):
    qt_spec = pl.BlockSpec((None, seq // tq, LANES, tq), lambda bi, g, qi: (bi, 0, g, 0))
    k_spec = pl.BlockSpec((None, seq, k_width), lambda bi, g, qi: (bi, 0, g))
    vt_spec = pl.BlockSpec((None, nk, vt_rows, tk), lambda bi, g, qi: (bi, 0, g, 0))
    o_spec = pl.BlockSpec((None, tq * TILES_PER_STEP, LANES), lambda bi, g, qi: (bi, qi, g))
    return qt_spec, k_spec, vt_spec, o_spec


def _pipelined_chunks(nk, lookahead, first_tile, scores, consume, s_scr, max_scr):
    unroll = PIPELINE_UNROLL if nk % PIPELINE_UNROLL == 0 else s_scr.shape[0]
    slots = s_scr.shape[0]
    assert nk % unroll == 0 and unroll % slots == 0 and lookahead < slots <= nk

    def produce(ahead, c, slot):
        s = scores(ahead, c)
        s_scr[slot] = s
        max_scr[slot] = jnp.max(s, axis=0, keepdims=True)

    @pl.when(first_tile)
    def _():
        for c in range(lookahead):
            produce(0, c, c)

    def body(i, carry):
        for j in range(unroll):
            c = unroll * i + j
            ahead = c + lookahead
            if j + lookahead < unroll:
                produce(0, ahead, (j + lookahead) % slots)
            else:
                wrap = int(ahead >= nk) if isinstance(ahead, int) else (ahead >= nk).astype(jnp.int32)
                produce(wrap, ahead - wrap * nk, (j + lookahead) % slots)
            consume(c, s_scr[j % slots], max_scr[j % slots])
        return carry

    if nk == unroll:
        body(0, 0)
    else:
        lax.fori_loop(0, nk // unroll, body, 0)


def _stage_query_tiles(qt_ref, qs_scr, qi, groups, extend=lambda stack: stack):
    last = qt_ref.shape[0] - 1

    @pl.when(qi == 0)
    def _():
        qs_scr[0] = extend(_stack_masked_t(qt_ref[0], groups))

    nxt = jnp.minimum(qi + 1, last)
    qs_scr[(qi + 1) % 2] = extend(_stack_masked_t(qt_ref[nxt], groups))
    return nxt


def _gqa_kernel(qt_ref, k_ref, vt_ref, o_ref, qs_scr, s_scr, max_scr, m_scr, acc_scr):
    tq = qt_ref.shape[2]
    for t in range(TILES_PER_STEP):
        _gqa_tile(pl.program_id(2) * TILES_PER_STEP + t, qt_ref, k_ref, vt_ref,
                  o_ref.at[t * tq:(t + 1) * tq, :], qs_scr, s_scr, max_scr, m_scr, acc_scr)


def _gqa_tile(qi, qt_ref, k_ref, vt_ref, o_ref, qs_scr, s_scr, max_scr, m_scr, acc_scr):
    tq = qt_ref.shape[2]
    nk, _, tk = vt_ref.shape
    _stage_query_tiles(qt_ref, qs_scr, qi, 2)
    _init_softmax_state(m_scr, acc_scr)

    def scores(ahead, c):
        start = pl.multiple_of(c * tk, tk)
        return jnp.dot(k_ref[pl.ds(start, tk), :], qs_scr[(qi + ahead) % 2],
                       preferred_element_type=F32)

    def consume(c, s, m_cur):
        _online_softmax_step(s, m_cur, [vt_ref[c]], m_scr, acc_scr)

    _pipelined_chunks(nk, A_LOOKAHEAD, qi == 0, scores, consume, s_scr, max_scr)
    o = _normalized(acc_scr)
    o_ref[...] = jnp.concatenate([o[:, :tq], o[:, tq:]], axis=0).T.astype(o_ref.dtype)


def _gqa_call(qt, k, vt):
    b, nk, vt_rows, tk = vt.shape
    vt_rows //= A_KV_HEADS
    seq = k.shape[1]
    tq = min(A_Q_TILE, seq)
    qt_spec, k_spec, vt_spec, o_spec = _dense_specs(b, seq, tq, nk, vt_rows, tk)
    return pl.pallas_call(
        _gqa_kernel,
        grid=(b, A_KV_HEADS, seq // (tq * TILES_PER_STEP)),
        in_specs=[qt_spec, k_spec, vt_spec],
        out_specs=o_spec,
        out_shape=jax.ShapeDtypeStruct((b, seq, MIXER_WIDTH), BF16),
        scratch_shapes=[pltpu.VMEM((2, LANES, 2 * tq), BF16),
                        pltpu.VMEM((SCORE_SLOTS, tk, 2 * tq), F32),
                        pltpu.VMEM((SCORE_SLOTS, 1, 2 * tq), F32),
                        pltpu.VMEM((1, 2 * tq), F32),
                        pltpu.VMEM((VT_ROWS, 2 * tq), F32)],
        compiler_params=pltpu.CompilerParams(
            dimension_semantics=("parallel", "parallel", "arbitrary"),
            vmem_limit_bytes=VMEM_LIMIT_BYTES),
        name="gqa_attention",
    )(qt, k, vt)


def _diff_kernel(scal_ref, qt_ref, k_ref, vt_ref, lq_ref, lk_ref, gsub_ref, o_ref,
                 qs_scr, s_scr, max_scr, key_scr, m_scr, acc_scr):
    tq = qt_ref.shape[2]
    for t in range(TILES_PER_STEP):
        _diff_tile(pl.program_id(2) * TILES_PER_STEP + t, scal_ref, qt_ref, k_ref, vt_ref,
                   lq_ref, lk_ref, gsub_ref, o_ref.at[t * tq:(t + 1) * tq, :],
                   qs_scr, s_scr, max_scr, key_scr, m_scr, acc_scr)


def _diff_tile(qi, scal_ref, qt_ref, k_ref, vt_ref, lq_ref, lk_ref, gsub_ref, o_ref,
               qs_scr, s_scr, max_scr, key_scr, m_scr, acc_scr):
    tq = qt_ref.shape[2]
    nk, _, tk = vt_ref.shape
    pair = pl.program_id(1)
    lam_init = scal_ref[0]
    slope_a = scal_ref[1 + 2 * pair]
    slope_b = scal_ref[2 + 2 * pair]
    q0 = qi * tq
    c_mid = q0 // tk
    col = lax.broadcasted_iota(jnp.int32, (1, 4 * tq), 1)
    col_slope = jnp.where(col < 2 * tq, slope_a, slope_b)
    col_pos = (q0 + col % tq).astype(F32)
    tiles_per_chunk = tk // tq

    s1 = col_slope.astype(BF16).astype(F32)
    s2 = (col_slope - s1).astype(BF16).astype(F32)
    s3 = (col_slope - s1 - s2).astype(BF16).astype(F32)
    frow = lax.broadcasted_iota(jnp.int32, (LANES, 4 * tq), 0)
    slope_rows = jnp.where(frow % 3 == 0, s1, jnp.where(frow % 3 == 1, s2, s3))
    slope_rows = jnp.where(frow < 6, slope_rows, 0.0)

    def extend(stack):
        return jnp.stack([jnp.concatenate([stack, (sg * slope_rows).astype(BF16)], axis=0)
                          for sg in (1.0, -1.0, 0.0)])

    nxt = _stage_query_tiles(qt_ref, qs_scr, qi, 4, extend)
    _init_softmax_state(m_scr, acc_scr)

    @pl.when(qi == 0)
    def _():
        rel = (lax.broadcasted_iota(jnp.int32, (tk, 4 * tq), 1) % tq
               - lax.broadcasted_iota(jnp.int32, (tk, 4 * tq), 0))
        for t in range(tiles_per_chunk):
            key_scr[t] = -(jnp.abs(rel + t * tq).astype(F32) * col_slope)

    def scores(ahead, j):
        assert isinstance(j, int)
        tile = jnp.where(ahead == 0, qi, nxt)
        mid = tile // tiles_per_chunk
        c = (mid + j) % nk
        variant = 2 if j == 0 else jnp.where(c < mid, 0, 1)
        start = pl.multiple_of(c * tk, tk)
        s = jnp.dot(k_ref[pl.ds(start, tk), :], qs_scr[(qi + ahead) % 2, variant],
                    preferred_element_type=F32)
        return s + key_scr[tile % tiles_per_chunk] if j == 0 else s

    def consume(j, s, m_cur):
        c = (c_mid + j) % nk
        vts = [vt_ref[c, :VT_ROWS, :], vt_ref[c, VT_ROWS:, :]]
        if j == 0:
            col_offset = None
        else:
            sign = jnp.where(c < c_mid, 1.0, -1.0).astype(F32)
            col_offset = col_slope * (sign * (col_pos - (c * tk).astype(F32)))
        _online_softmax_step(s, m_cur, vts, m_scr, acc_scr, col_offset)

    _pipelined_chunks(nk, B_LOOKAHEAD, qi == 0, scores, consume, s_scr, max_scr)

    t = jnp.exp(jnp.sum(lq_ref[...] * lk_ref[...], axis=1, keepdims=True))
    lam = t[0:1, :] - t[1:2, :] + lam_init
    o = _normalized(acc_scr)
    o_a = o[:, 0:tq] - lam * o[:, tq:2 * tq]
    o_b = o[:, 2 * tq:3 * tq] - lam * o[:, 3 * tq:]
    def head_rms(y):
        return y * lax.rsqrt(jnp.mean(y * y, axis=0, keepdims=True) + RMS_EPS)

    o = jnp.concatenate([head_rms(o_a), head_rms(o_b)], axis=0).T
    o_ref[...] = (o * gsub_ref[...] * (1.0 - lam_init)).astype(o_ref.dtype)


def _diff_call(scal, qt, k, vt, lq, lk, gsub):
    b, nk, vt_rows, tk = vt.shape
    vt_rows //= B_HEADS // 2
    seq = k.shape[1]
    tq = min(B_Q_TILE, seq)
    assert tk % tq == 0
    qt_spec, k_spec, vt_spec, o_spec = _dense_specs(b, seq, tq, nk, vt_rows, tk, 2 * LANES)
    full = lambda a: pl.BlockSpec(a.shape, lambda bi, g, qi: (0,) * a.ndim)
    return pl.pallas_call(
        _diff_kernel,
        grid=(b, B_HEADS // 2, seq // (tq * TILES_PER_STEP)),
        in_specs=[pl.BlockSpec(memory_space=pltpu.SMEM), qt_spec, k_spec, vt_spec,
                  full(lq), full(lk), full(gsub)],
        out_specs=o_spec,
        out_shape=jax.ShapeDtypeStruct((b, seq, MIXER_WIDTH), BF16),
        scratch_shapes=[pltpu.VMEM((2, 3, 2 * LANES, 4 * tq), BF16),
                        pltpu.VMEM((SCORE_SLOTS, tk, 4 * tq), F32),
                        pltpu.VMEM((SCORE_SLOTS, 1, 4 * tq), F32),
                        pltpu.VMEM((tk // tq, tk, 4 * tq), F32),
                        pltpu.VMEM((1, 4 * tq), F32),
                        pltpu.VMEM((VT_ROWS, 4 * tq), F32)],
        compiler_params=pltpu.CompilerParams(
            dimension_semantics=("parallel", "parallel", "arbitrary"),
            vmem_limit_bytes=VMEM_LIMIT_BYTES),
        name="diff_attention",
    )(scal, qt, k, vt, lq, lk, gsub)


def _nbr_kernel(q_ref, kp_ref, kc_ref, kn_ref, vp_ref, vc_ref, vn_ref, bias_ref, o_ref,
                kband, vband, *, grid_rows):
    blk = q_ref.shape[0]
    rb = blk // GRID_W
    i = pl.program_id(1)
    kband[0:blk] = kp_ref[...]
    kband[blk:2 * blk] = kc_ref[...]
    kband[2 * blk:] = kn_ref[...]
    vband[0:blk] = vp_ref[...]
    vband[blk:2 * blk] = vc_ref[...]
    vband[2 * blk:] = vn_ref[...]
    band = NA_ROWS * GRID_W
    lane_head = lax.broadcasted_iota(jnp.int32, (GRID_W, MIXER_WIDTH), 1) // HEAD_DIM
    for jr in range(rb):
        r = i * rb + jr
        rs = jnp.clip(r - NA_ROWS // 2, 0, grid_rows - NA_ROWS)
        start = pl.multiple_of((rs - (i - 1) * rb) * GRID_W, GRID_W)
        k = kband[pl.ds(start, band), :]
        v = vband[pl.ds(start, band), :]
        qs = _stack_masked(q_ref[jr * GRID_W:(jr + 1) * GRID_W, :], C_HEADS)
        s = lax.dot_general(qs, k, (((1,), (1,)), ((), ())), preferred_element_type=F32)
        s = s + bias_ref[r - rs].reshape(C_HEADS * GRID_W, band)
        m = jnp.max(s, axis=1, keepdims=True)
        p = jnp.exp2(s - m)
        l = jnp.sum(p, axis=1, keepdims=True)
        pv = jnp.dot(p.astype(BF16), v, preferred_element_type=F32) / l
        o = jnp.zeros((GRID_W, MIXER_WIDTH), F32)
        for hd in range(C_HEADS):
            o = jnp.where(lane_head == hd, pv[hd * GRID_W:(hd + 1) * GRID_W], o)
        o_ref[jr * GRID_W:(jr + 1) * GRID_W, :] = o.astype(o_ref.dtype)


def _nbr_call(q, k, v, bias):
    b, seq, w = q.shape
    grid_rows = seq // GRID_W
    rb = min(C_BLOCK_ROWS, grid_rows)
    blk = rb * GRID_W
    nblk = seq // blk
    cur = pl.BlockSpec((None, blk, w), lambda bi, i: (bi, i, 0))
    prev = pl.BlockSpec((None, blk, w), lambda bi, i: (bi, jnp.maximum(i - 1, 0), 0))
    nxt = pl.BlockSpec((None, blk, w), lambda bi, i: (bi, jnp.minimum(i + 1, nblk - 1), 0))
    return pl.pallas_call(
        functools.partial(_nbr_kernel, grid_rows=grid_rows),
        grid=(b, nblk),
        in_specs=[cur, prev, cur, nxt, prev, cur, nxt,
                  pl.BlockSpec(bias.shape, lambda bi, i: (0, 0, 0, 0))],
        out_specs=cur,
        out_shape=jax.ShapeDtypeStruct((b, seq, w), BF16),
        scratch_shapes=[pltpu.VMEM((3 * blk, w), BF16), pltpu.VMEM((3 * blk, w), BF16)],
        compiler_params=pltpu.CompilerParams(
            dimension_semantics=("parallel", "parallel"), vmem_limit_bytes=VMEM_LIMIT_BYTES),
        name="nbr_attention",
    )(q, k, k, k, v, v, v, bias)


def _nbr_bias_table(rpb):
    qc = np.arange(GRID_W)
    kc = np.arange(GRID_W)
    cs = np.clip(qc - NA_COLS // 2, 0, GRID_W - NA_COLS)
    valid = (kc[None, :] >= cs[:, None]) & (kc[None, :] < cs[:, None] + NA_COLS)
    dc = kc[None, :] - qc[:, None] + (NA_COLS - 1)
    d = np.arange(NA_ROWS)
    a = np.arange(NA_ROWS)
    dr = a[None, :] - d[:, None] + (NA_ROWS - 1)
    pick_r = (dr[:, :, None] == np.arange(2 * NA_ROWS - 1)).astype(np.float32)
    pick_c = ((dc[:, :, None] == np.arange(2 * NA_COLS - 1)) & valid[:, :, None]).astype(np.float32)
    tbl = jnp.einsum("hrc,dar,qkc->dhqak", rpb.astype(F32) * LOG2E, pick_r, pick_c,
                     precision=lax.Precision.HIGHEST)
    tbl = jnp.where(valid[None, None, :, None, :], tbl, NEG_BIG)
    return tbl.reshape(NA_ROWS, C_HEADS, GRID_W, NA_ROWS * GRID_W)


def _channel_kernel(h_ref, ya_ref, yb_ref, yc_ref, yd_ref, wout_ref, gffn_ref, wg_ref, wu_ref,
                    wd_ref, gple_ref, wpg_ref, p_ref, wpp_ref, gfin_ref, o_ref, *, final):
    w = MIXER_WIDTH
    h = h_ref[...]
    for j, y_ref in enumerate((ya_ref, yb_ref, yc_ref, yd_ref)):
        h = h + jnp.dot(y_ref[...], wout_ref[j * w:(j + 1) * w, :], preferred_element_type=F32)

    hn = _rms(h, gffn_ref[...]).astype(BF16)
    ffn = wg_ref.shape[1]
    acc = jnp.zeros(h.shape, F32)
    for c in range(ffn // FFN_CHUNK):
        cols = slice(c * FFN_CHUNK, (c + 1) * FFN_CHUNK)
        gate = jnp.dot(hn, wg_ref[:, cols], preferred_element_type=F32)
        up = jnp.dot(hn, wu_ref[:, cols], preferred_element_type=F32)
        act = (gate * jax.nn.sigmoid(gate) * up).astype(BF16)
        acc = acc + jnp.dot(act, wd_ref[cols, :], preferred_element_type=F32)
    h = h + acc

    hn = _rms(h, gple_ref[...]).astype(BF16)
    gate = jax.nn.sigmoid(jnp.dot(hn, wpg_ref[...], preferred_element_type=F32))
    emb = jnp.dot(p_ref[...].astype(BF16), wpp_ref[...], preferred_element_type=F32)
    h = h + gate * emb
    if final:
        h = _rms(h, gfin_ref[...])
    o_ref[...] = h


def _channel_call(h, ya, yb, yc, yd, wout, gffn, wg, wu, wd, gple, wpg, p, wpp, gfin, final, seq):
    rows, d = h.shape
    tm = min(ROW_TILE, seq)
    row_spec = lambda w: pl.BlockSpec((tm, w), lambda i: (i, 0))
    resident = lambda a: pl.BlockSpec(a.shape, lambda i: (0,) * a.ndim,
                                      pipeline_mode=pl.Buffered(1))
    return pl.pallas_call(
        functools.partial(_channel_kernel, final=final),
        grid=(rows // tm,),
        in_specs=[row_spec(d)] + [row_spec(MIXER_WIDTH)] * 4
                 + [resident(wout), resident(gffn), resident(wg), resident(wu), resident(wd),
                    resident(gple), resident(wpg), row_spec(p.shape[1]), resident(wpp),
                    resident(gfin)],
        out_specs=row_spec(d),
        out_shape=jax.ShapeDtypeStruct((rows, d), F32),
        compiler_params=pltpu.CompilerParams(
            dimension_semantics=("parallel",), vmem_limit_bytes=VMEM_LIMIT_BYTES),
        name="channel_mixers",
    )(h, ya, yb, yc, yd, wout, gffn, wg, wu, wd, gple, wpg, p, wpp, gfin)


def _rope_tables(seq):
    t = jnp.arange(seq)
    row = (t // GRID_W).astype(F32)
    col = (t % GRID_W).astype(F32)
    n_freq = HEAD_DIM // 4
    inv = ROPE_THETA ** (-jnp.arange(n_freq, dtype=F32) / n_freq)
    ang = jnp.concatenate([row[:, None] * inv, col[:, None] * inv], axis=-1)
    cos, sin = jnp.cos(ang), jnp.sin(ang)
    heads = LANES // HEAD_DIM
    return (jnp.tile(jnp.concatenate([cos, cos], axis=-1), (1, heads)),
            jnp.tile(jnp.concatenate([-sin, sin], axis=-1), (1, heads)))


def kernel(x, p, g_mix, w_in, a_q_norm, a_k_norm, b_lam_q, b_lam_k, b_sub_norm, c_rpb, d_ln_g,
           d_ln_b, d_w_s, d_b_s, w_out, g_ffn, w_gate, w_up, w_down, g_ple, w_ple_gate,
           w_ple_proj, g_final):
    b, seq, d = x.shape
    depth = w_in.shape[0]
    assert seq % GRID_W == 0 and seq // GRID_W >= NA_ROWS and seq % D_CHUNK == 0
    rows = b * seq
    h = x.reshape(rows, d)
    cos_t, sin_t = _rope_tables(seq)
    slopes = (2.0 ** (-8.0 / B_HEADS)) ** jnp.arange(1, B_HEADS + 1, dtype=F32) * LOG2E
    row2 = lambda a: a.reshape(1, -1).astype(F32)
    for i in range(depth):
        bs_tile = jnp.repeat(d_b_s[i].T, HEAD_DIM, axis=1)
        aqt, ak, avt, bqt, bk, bvt, cq, ck, cv, yd = _proj_call(
            h, row2(g_mix[i]), w_in[i].astype(BF16),
            row2(jnp.tile(a_q_norm[i], A_HEADS)), row2(jnp.tile(a_k_norm[i], A_KV_HEADS)),
            cos_t, sin_t, row2(d_ln_g[i]), row2(d_ln_b[i]), d_w_s[i].astype(BF16),
            bs_tile.astype(F32), seq)
        sh = lambda a: a.reshape(b, seq, MIXER_WIDTH)
        vt4 = lambda a: a.reshape(b, -1, a.shape[-2], a.shape[-1])
        ya = _gqa_call(vt4(aqt), sh(ak), vt4(avt))
        lam_init = 0.8 - 0.6 * math.exp(-0.3 * i)
        scal = jnp.concatenate([jnp.full((1,), lam_init, F32), slopes])
        yb = _diff_call(scal, vt4(bqt), bk.reshape(b, seq, -1), vt4(bvt), b_lam_q[i].astype(F32),
                        b_lam_k[i].astype(F32), row2(jnp.tile(b_sub_norm[i], LANES // HEAD_DIM)))
        yc = _nbr_call(sh(cq), sh(ck), sh(cv), _nbr_bias_table(c_rpb[i]))
        fl = lambda a: a.reshape(rows, MIXER_WIDTH)
        h = _channel_call(
            h, fl(ya), fl(yb), fl(yc), yd, w_out[i].astype(BF16), row2(g_ffn[i]),
            w_gate[i].astype(BF16), w_up[i].astype(BF16), w_down[i].astype(BF16),
            row2(g_ple[i]), w_ple_gate[i].astype(BF16), p[i].reshape(rows, -1),
            w_ple_proj[i].astype(BF16), row2(g_final), i == depth - 1, seq)
    return h.reshape(b, seq, d)
```
